```python
import jax, jax.numpy as jnp
from jax import lax
import numpy as np

D_MODEL = 1024
BATCH = 32
SEQ = 256
DEPTH = 2
DEC_BATCH = 8
DEC_SEQ = 4096
PAST_LEN = 512

GRID_W = 64
MLA_HEADS = 8
NOPE_DIM = 64
ROPE_DIM = 32
ROPE_FREQS = ROPE_DIM // 4
V_DIM = 64
Q_LORA = 256
KV_LORA = 128
ROPE_BASE = 10000.0
Q_BLOCK = 128
SM_SCALE = (NOPE_DIM + ROPE_DIM) ** -0.5
POOL_WINDOWS = (2, 4, 8, 16)
POOL_GC = 64
POOL_W = POOL_GC * len(POOL_WINDOWS)
SGU_HEADS = 4
SGU_HD = 64
SGU_W = SGU_HEADS * SGU_HD
CHUNK = 128
IN_COLS = Q_LORA + KV_LORA + ROPE_DIM + POOL_W + 2 * SGU_W
MIX_W = MLA_HEADS * V_DIM + POOL_W + SGU_W
D_FF = 2816
CONV_W = 3
EPS = 1e-6

kernel_name = "hymba_mla_pool_sgu_diffusion_step"


def _rms(x, g):
    xf = x.astype(jnp.float32)
    y = xf * lax.rsqrt(jnp.mean(xf * xf, axis=-1, keepdims=True) + EPS)
    return (y * g.astype(jnp.float32)).astype(x.dtype)


def _axial_angles(S):
    n_rows = S // GRID_W
    rows = jnp.repeat(jnp.arange(n_rows, dtype=jnp.float32), GRID_W)
    cols = jnp.tile(jnp.arange(GRID_W, dtype=jnp.float32), n_rows)
    freqs = ROPE_BASE ** (-jnp.arange(ROPE_FREQS, dtype=jnp.float32) / ROPE_FREQS)
    ang = jnp.stack([rows[:, None] * freqs, cols[:, None] * freqs], axis=1)
    return jnp.cos(ang), jnp.sin(ang)


def _rope2d(x, cos, sin):
    xr = x.reshape(*x.shape[:-1], 2, 2, ROPE_FREQS)
    x1 = xr[..., 0, :].astype(jnp.float32)
    x2 = xr[..., 1, :].astype(jnp.float32)
    out = jnp.stack([x1 * cos - x2 * sin, x2 * cos + x1 * sin], axis=-2)
    return out.reshape(x.shape).astype(x.dtype)


def _block_attention(q_nope, q_rope, k_nope, k_rope, v):
    B, S, H, _ = q_nope.shape
    nb = S // Q_BLOCK
    qn = q_nope.reshape(B, nb, Q_BLOCK, H, NOPE_DIM).transpose(1, 0, 2, 3, 4)
    qr = q_rope.reshape(B, nb, Q_BLOCK, H, ROPE_DIM).transpose(1, 0, 2, 3, 4)

    def one(blk):
        qn_b, qr_b = blk
        s = (jnp.einsum('bqhd,bkhd->bhqk', qn_b, k_nope)
             + jnp.einsum('bqhr,bkr->bhqk', qr_b, k_rope)).astype(jnp.float32) * SM_SCALE
        p = jax.nn.softmax(s, axis=-1).astype(v.dtype)
        return jnp.einsum('bhqk,bkhd->bqhd', p, v)

    o = lax.map(one, (qn, qr))
    return o.transpose(1, 0, 2, 3, 4).reshape(B, S, H * V_DIM)


def _multiscale_pool(x, w_pool, pool_scale):
    B, S, _ = x.shape
    xf = x.astype(jnp.float32)
    cs = jnp.concatenate([jnp.zeros((B, 1, POOL_W), jnp.float32), jnp.cumsum(xf, axis=1)], axis=1)
    t = jnp.arange(S)
    outs = []
    for g, w in enumerate(POOL_WINDOWS):
        lo = jnp.clip(t - w // 2, 0, S)
        hi = jnp.clip(t + w - w // 2, 0, S)
        seg = cs[:, :, g * POOL_GC:(g + 1) * POOL_GC]
        tot = jnp.take(seg, hi, axis=1) - jnp.take(seg, lo, axis=1)
        mean = tot / (hi - lo).astype(jnp.float32)[None, :, None]
        outs.append(mean - xf[:, :, g * POOL_GC:(g + 1) * POOL_GC])
    d = jnp.stack(outs, axis=2).astype(x.dtype)
    y = jnp.einsum('bsgc,gce->bsge', d, w_pool).reshape(B, S, POOL_W)
    return y * pool_scale


def _spatial_gate(x, g_sgu, w_sgu, b_sgu):
    u, v = x[..., :SGU_W], x[..., SGU_W:]
    v = _rms(v, g_sgu)
    B, S, _ = v.shape
    nc = S // CHUNK
    v = v.reshape(B, nc, CHUNK, SGU_HEADS, SGU_HD)
    z = jnp.einsum('hqp,bnphd->bnqhd', w_sgu, v) + b_sgu.T[None, None, :, :, None]
    return u * z.reshape(B, S, SGU_W)


def _mixer(h, p, latent, ctx_ckv, ctx_krope):
    B, S, _ = h.shape
    proj = jnp.einsum('bsd,de->bse', h, p['w_in'])
    o1 = Q_LORA
    o2 = o1 + KV_LORA
    o3 = o2 + ROPE_DIM
    o4 = o3 + POOL_W
    qa, kva, kr, pool_in, sgu_in = proj[..., :o1], proj[..., o1:o2], proj[..., o2:o3], proj[..., o3:o4], proj[..., o4:]
    q = jnp.einsum('bsr,re->bse', _rms(qa, p['g_q_a']), p['w_q_b']).reshape(B, S, MLA_HEADS, NOPE_DIM + ROPE_DIM)
    q_nope, q_rope = q[..., :NOPE_DIM], q[..., NOPE_DIM:]
    ckv = _rms(kva, p['g_kv_a'])
    if latent:
        cos, sin = _axial_angles(S)
        q_rope = _rope2d(q_rope, cos[:, None], sin[:, None])
        kr_lat = _rope2d(kr, cos, sin)
        ckv_all = jnp.concatenate([ctx_ckv, ckv], axis=1)
        kr_all = jnp.concatenate([ctx_krope, kr_lat], axis=1)
    else:
        ckv_all, kr_all = ckv, kr
    T = ckv_all.shape[1]
    kv = jnp.einsum('btr,re->bte', ckv_all, p['w_kv_b']).reshape(B, T, MLA_HEADS, NOPE_DIM + V_DIM)
    k_nope, v = kv[..., :NOPE_DIM], kv[..., NOPE_DIM:]
    attn = _block_attention(q_nope, q_rope, k_nope, kr_all, v)
    pool = _multiscale_pool(pool_in, p['w_pool'], p['pool_scale'])
    sgu = _spatial_gate(sgu_in, p['g_sgu'], p['w_sgu'], p['b_sgu'])
    out = jnp.einsum('bse,ed->bsd', jnp.concatenate([attn, pool, sgu], axis=-1), p['w_out'])
    return out, ckv, kr


def _conv_ffn(h, w_up, conv_w, conv_b, w_down):
    z = jnp.einsum('bsd,df->bsf', h, w_up)
    zp = jnp.pad(z, ((0, 0), (1, 1), (0, 0)))
    z = zp[:, :-2] * conv_w[0] + zp[:, 1:-1] * conv_w[1] + zp[:, 2:] * conv_w[2] + conv_b
    g, val = z[..., :D_FF], z[..., D_FF:]
    return jnp.einsum('bsf,fd->bsd', jax.nn.silu(g) * val, w_down)


def _layer(x, mod, p, latent, ctx_ckv, ctx_krope):
    shift_m, scale_m, gate_m, shift_f, scale_f, gate_f = jnp.split(mod, 6, axis=-1)
    h = _rms(x, p['g_pre_mix']) * (1 + scale_m) + shift_m
    out, ckv, kr = _mixer(h, p, latent, ctx_ckv, ctx_krope)
    x = x + gate_m * _rms(out, p['g_post_mix'])
    h = _rms(x, p['g_pre_ffn']) * (1 + scale_f) + shift_f
    x = x + gate_f * _rms(_conv_ffn(h, p['w_up'], p['conv_w'], p['conv_b'], p['w_down']), p['g_post_ffn'])
    return x, ckv, kr


def setup_inputs(seed: int = 0) -> dict:
    key = jax.random.key(seed)
    ks = jax.random.split(key, 32)
    nrm = lambda k, shp, s: jax.random.normal(k, shp, jnp.float32) * s
    gain = lambda k, n: 1.0 + nrm(k, (DEPTH, n), 0.02)
    return {
        "x_prompt": nrm(ks[0], (BATCH, SEQ, D_MODEL), 1.0),
        "x_sample": nrm(ks[1], (DEC_BATCH, DEC_SEQ, D_MODEL), 1.0),
        "cache_ckv": nrm(ks[2], (DEC_BATCH, DEPTH, PAST_LEN, KV_LORA), 1.0),
        "cache_krope": nrm(ks[3], (DEC_BATCH, DEPTH, PAST_LEN, ROPE_DIM), 1.0),
        "c": nrm(ks[4], (DEC_BATCH, D_MODEL), 1.0),
        "c_ctx": nrm(ks[5], (D_MODEL,), 1.0),
        "w_mod": nrm(ks[6], (DEPTH, D_MODEL, 6 * D_MODEL), D_MODEL ** -0.5),
        "b_mod": nrm(ks[7], (DEPTH, 6 * D_MODEL), 0.02),
        "g_pre_mix": gain(ks[8], D_MODEL),
        "g_post_mix": gain(ks[9], D_MODEL),
        "g_pre_ffn": gain(ks[10], D_MODEL),
        "g_post_ffn": gain(ks[11], D_MODEL),
        "w_in": nrm(ks[12], (DEPTH, D_MODEL, IN_COLS), D_MODEL ** -0.5),
        "g_q_a": gain(ks[13], Q_LORA),
        "w_q_b": nrm(ks[14], (DEPTH, Q_LORA, MLA_HEADS * (NOPE_DIM + ROPE_DIM)), Q_LORA ** -0.5),
        "g_kv_a": gain(ks[15], KV_LORA),
        "w_kv_b": nrm(ks[16], (DEPTH, KV_LORA, MLA_HEADS * (NOPE_DIM + V_DIM)), KV_LORA ** -0.5),
        "w_pool": nrm(ks[17], (DEPTH, len(POOL_WINDOWS), POOL_GC, POOL_GC), POOL_GC ** -0.5),
        "pool_scale": 1.0 + nrm(ks[18], (DEPTH, POOL_W), 0.1),
        "g_sgu": gain(ks[19], SGU_W),
        "w_sgu": nrm(ks[20], (DEPTH, SGU_HEADS, CHUNK, CHUNK), CHUNK ** -0.5),
        "b_sgu": 1.0 + nrm(ks[21], (DEPTH, SGU_HEADS, CHUNK), 0.02),
        "w_out": nrm(ks[22], (DEPTH, MIX_W, D_MODEL), MIX_W ** -0.5),
        "w_up": nrm(ks[23], (DEPTH, D_MODEL, 2 * D_FF), D_MODEL ** -0.5),
        "conv_w": nrm(ks[24], (DEPTH, CONV_W, 2 * D_FF), CONV_W ** -0.5),
        "conv_b": nrm(ks[25], (DEPTH, 2 * D_FF), 0.02),
        "w_down": nrm(ks[26], (DEPTH, D_FF, D_MODEL), D_FF ** -0.5),
    }


def reference(x_prompt, x_sample, cache_ckv, cache_krope, c, c_ctx, w_mod, b_mod,
              g_pre_mix, g_post_mix, g_pre_ffn, g_post_ffn, w_in, g_q_a, w_q_b, g_kv_a, w_kv_b,
              w_pool, pool_scale, g_sgu, w_sgu, b_sgu, w_out, w_up, conv_w, conv_b, w_down):
    xp, xs = x_prompt, x_sample
    ckv_list, kr_list = [], []
    for l in range(DEPTH):
        p = {
            'g_pre_mix': g_pre_mix[l], 'g_post_mix': g_post_mix[l],
            'g_pre_ffn': g_pre_ffn[l], 'g_post_ffn': g_post_ffn[l],
            'w_in': w_in[l], 'g_q_a': g_q_a[l], 'w_q_b': w_q_b[l],
            'g_kv_a': g_kv_a[l], 'w_kv_b': w_kv_b[l],
            'w_pool': w_pool[l], 'pool_scale': pool_scale[l],
            'g_sgu': g_sgu[l], 'w_sgu': w_sgu[l], 'b_sgu': b_sgu[l],
            'w_out': w_out[l], 'w_up': w_up[l], 'conv_w': conv_w[l],
            'conv_b': conv_b[l], 'w_down': w_down[l],
        }
        mod_ctx = (jnp.einsum('d,de->e', jax.nn.silu(c_ctx), w_mod[l]) + b_mod[l])[None, None, :]
        mod_lat = (jnp.einsum('bd,de->be', jax.nn.silu(c), w_mod[l]) + b_mod[l])[:, None, :]
        xp, ckv, kr = _layer(xp, mod_ctx, p, False, None, None)
        ckv_list.append(ckv)
        kr_list.append(kr)
        xs, _, _ = _layer(xs, mod_lat, p, True, cache_ckv[:, l], cache_krope[:, l])
    new_ckv = jnp.stack(ckv_list, axis=1)
    new_krope = jnp.stack(kr_list, axis=1)
    return (xp, xs, new_ckv, new_krope)
```

```python
import functools

import jax
import jax.numpy as jnp
from jax import lax
from jax.experimental import pallas as pl
from jax.experimental.pallas import tpu as pltpu

F32 = jnp.float32
BF16 = jnp.bfloat16

GRID_W = 64
MLA_HEADS = 8
NOPE_DIM = 64
ROPE_DIM = 32
ROPE_FREQS = ROPE_DIM // 4
V_DIM = 64
Q_LORA = 256
KV_LORA = 128
ROPE_BASE = 10000.0
SM_SCALE = (NOPE_DIM + ROPE_DIM) ** -0.5
POOL_WINDOWS = (2, 4, 8, 16)
POOL_GC = 64
POOL_W = POOL_GC * len(POOL_WINDOWS)
SGU_HEADS = 4
SGU_HD = 64
SGU_W = SGU_HEADS * SGU_HD
CHUNK = 128
EPS = 1e-6

LANES = 128
BF16_SUBLANES = 16
HEAD_PAD = LANES
VMEM_LIMIT = 56 * 1024 * 1024

C_QA = 0
C_KVA = C_QA + Q_LORA
C_POOL = C_KVA + KV_LORA
C_U = C_POOL + POOL_W
C_V = C_U + SGU_W
C_KR = C_V + SGU_W
C_KRSW = C_KR + LANES
IN_COLS_PAD = C_KRSW + LANES

FF_CHUNK = 256
POOL_PAD = 16
POOL_ROWS = 256


def _rms(x, g):
    return x * lax.rsqrt(jnp.mean(x * x, axis=-1, keepdims=True) + EPS) * g


def _dot(a, b):
    return jnp.dot(a, b, preferred_element_type=F32)


def _params(sem):
    return pltpu.CompilerParams(dimension_semantics=sem, vmem_limit_bytes=VMEM_LIMIT)


def _const_spec(shape):
    nd = len(shape)
    return pl.BlockSpec(shape, lambda *_: (0,) * nd)


def _mod_kernel(c_ref, w_ref, b_ref, o_ref):
    a = jax.nn.silu(c_ref[...]).astype(BF16)
    o_ref[0] = _dot(a, w_ref[0].astype(BF16)) + b_ref[0]


def _modulation(cc, w_mod, b_mod):
    depth, d, e = w_mod.shape
    rows = cc.shape[0]
    te = 1536
    return pl.pallas_call(
        _mod_kernel,
        out_shape=jax.ShapeDtypeStruct((depth, rows, e), F32),
        grid=(depth, e // te),
        in_specs=[
            pl.BlockSpec((rows, d), lambda l, j: (0, 0)),
            pl.BlockSpec((1, d, te), lambda l, j: (l, 0, j)),
            pl.BlockSpec((1, 1, te), lambda l, j: (l, 0, j)),
        ],
        out_specs=pl.BlockSpec((1, rows, te), lambda l, j: (l, 0, j)),
        compiler_params=_params(("parallel", "parallel")),
        name="modulation",
    )(cc, w_mod, b_mod.reshape(depth, 1, e))


def _pre_mixer_kernel(*refs, latent, tm):
    if latent:
        (x_ref, mod_ref, gpre_ref, win_ref, gqa_ref, wqb_ref, gkva_ref, gsgu_ref, wsgu_ref, bsgu_ref,
         cq_ref, sq_ref, ck_ref, sk_ref, q_ref, ckv_ref, kr_ref, pool_ref, sgu_ref) = refs
    else:
        (x_ref, mod_ref, gpre_ref, win_ref, gqa_ref, wqb_ref, gkva_ref, gsgu_ref, wsgu_ref, bsgu_ref,
         q_ref, ckv_ref, kr_ref, pool_ref, sgu_ref) = refs
    x = x_ref[0]
    shift = mod_ref[0, 0:1, :]
    scale = mod_ref[0, 1:2, :]
    h = _rms(x, gpre_ref[...]) * (1 + scale) + shift
    proj = _dot(h.astype(BF16), win_ref[...])

    qn = _rms(proj[:, C_QA:C_QA + Q_LORA], gqa_ref[...]).astype(BF16)
    qq = _dot(qn, wqb_ref[...])
    width = MLA_HEADS * HEAD_PAD
    for hh in range(MLA_HEADS):
        lo = hh * HEAD_PAD
        if latent:
            qh = qq[:, lo:lo + HEAD_PAD] * cq_ref[...] + qq[:, width + lo:width + lo + HEAD_PAD] * sq_ref[...]
        else:
            qh = qq[:, lo:lo + HEAD_PAD] * SM_SCALE
        q_ref[0, :, lo:lo + HEAD_PAD] = qh.astype(BF16)

    ckv_ref[0] = _rms(proj[:, C_KVA:C_KVA + KV_LORA], gkva_ref[...])
    kr = proj[:, C_KR:C_KR + ROPE_DIM]
    if latent:
        kr = kr * ck_ref[...] + proj[:, C_KRSW:C_KRSW + ROPE_DIM] * sk_ref[...]
    kr_ref[0] = kr
    pool_ref[0] = proj[:, C_POOL:C_POOL + POOL_W]

    u = proj[:, C_U:C_U + SGU_W]
    vn = _rms(proj[:, C_V:C_V + SGU_W], gsgu_ref[...]).astype(BF16)
    lane = lax.broadcasted_iota(jnp.int32, (1, LANES), 1)
    zero = jnp.zeros((), BF16)
    for c in range(tm // CHUNK):
        r0 = c * CHUNK
        for pair in range(SGU_W // LANES):
            l0 = pair * LANES
            vp = vn[r0:r0 + CHUNK, l0:l0 + LANES]
            v_lo = jnp.where(lane < SGU_HD, vp, zero)
            v_hi = jnp.where(lane >= SGU_HD, vp, zero)
            z = _dot(wsgu_ref[2 * pair], v_lo) + _dot(wsgu_ref[2 * pair + 1], v_hi) + bsgu_ref[:, l0:l0 + LANES]
            sgu_ref[0, r0:r0 + CHUNK, l0:l0 + LANES] = (u[r0:r0 + CHUNK, l0:l0 + LANES] * z).astype(BF16)


def _pre_mixer(x, mod, lw, tabs, *, latent, tm):
    B, S, D = x.shape
    mod_idx = (lambda b, i: (b + 1, 0, 0)) if latent else (lambda b, i: (0, 0, 0))
    tok = lambda b, i: (b, i, 0)
    in_specs = [
        pl.BlockSpec((1, tm, D), tok),
        pl.BlockSpec((1, 8, D), mod_idx),
        _const_spec((1, D)),
        _const_spec(lw['w_in'].shape),
        _const_spec((1, Q_LORA)),
        _const_spec(lw['w_q_b'].shape if latent else (Q_LORA, MLA_HEADS * HEAD_PAD)),
        _const_spec((1, KV_LORA)),
        _const_spec((1, SGU_W)),
        _const_spec((SGU_HEADS, CHUNK, CHUNK)),
        _const_spec((CHUNK, SGU_W)),
    ]
    args = [x, mod, lw['g_pre_mix'], lw['w_in'], lw['g_q_a'], lw['w_q_b'], lw['g_kv_a'], lw['g_sgu'],
            lw['w_sgu'], lw['b_sgu']]
    if latent:
        seq = lambda b, i: (i, 0)
        in_specs += [pl.BlockSpec((tm, HEAD_PAD), seq), pl.BlockSpec((tm, HEAD_PAD), seq),
                     pl.BlockSpec((tm, ROPE_DIM), seq), pl.BlockSpec((tm, ROPE_DIM), seq)]
        args += list(tabs)
    out_shape = [
        jax.ShapeDtypeStruct((B, S, MLA_HEADS * HEAD_PAD), BF16),
        jax.ShapeDtypeStruct((B, S, KV_LORA), F32),
        jax.ShapeDtypeStruct((B, S, ROPE_DIM), F32),
        jax.ShapeDtypeStruct((B, S, POOL_W), F32),
        jax.ShapeDtypeStruct((B, S, SGU_W), BF16),
    ]
    out_specs = [
        pl.BlockSpec((1, tm, MLA_HEADS * HEAD_PAD), tok),
        pl.BlockSpec((1, tm, KV_LORA), tok),
        pl.BlockSpec((1, tm, ROPE_DIM), tok),
        pl.BlockSpec((1, tm, POOL_W), tok),
        pl.BlockSpec((1, tm, SGU_W), tok),
    ]
    return pl.pallas_call(
        functools.partial(_pre_mixer_kernel, latent=latent, tm=tm),
        out_shape=out_shape,
        grid=(B, S // tm),
        in_specs=in_specs,
        out_specs=out_specs,
        compiler_params=_params(("parallel", "parallel")),
        name="pre_mixer_lat" if latent else "pre_mixer_ctx",
    )(*args)


def _pool_kernel(x_ref, w_ref, scale_ref, o_ref, pad_ref, *, S):
    zeros = jnp.zeros((POOL_PAD, POOL_W), F32)
    pad_ref[0:POOL_PAD, :] = zeros
    pad_ref[POOL_PAD + S:POOL_PAD + S + POOL_PAD, :] = zeros
    pad_ref[POOL_PAD:POOL_PAD + S, :] = x_ref[0]

    R = min(POOL_ROWS, S)
    lane = lax.broadcasted_iota(jnp.int32, (1, LANES), 1)
    lane_w = lax.broadcasted_iota(jnp.int32, (1, POOL_W), 1)
    half = jnp.where(lane_w < POOL_GC, POOL_WINDOWS[0] // 2,
                     jnp.where(lane_w < 2 * POOL_GC, POOL_WINDOWS[1] // 2,
                               jnp.where(lane_w < 3 * POOL_GC, POOL_WINDOWS[2] // 2, POOL_WINDOWS[3] // 2)))
    for r in range(S // R):
        base = POOL_PAD + r * R

        def ld(off, lo):
            return pad_ref[base + off:base + off + R, lo:lo + LANES]

        x0 = ld(0, 0)
        s2 = ld(-1, 0) + x0
        s4 = s2 + ld(-2, 0) + ld(1, 0)
        tot_a = jnp.where(lane < POOL_GC, s2, s4)
        x1 = ld(0, LANES)
        s8 = x1
        for off in (-4, -3, -2, -1, 1, 2, 3):
            s8 = s8 + ld(off, LANES)
        s16 = s8
        for off in (-8, -7, -6, -5, 4, 5, 6, 7):
            s16 = s16 + ld(off, LANES)
        tot_b = jnp.where(lane < POOL_GC, s8, s16)
        tot = jnp.concatenate([tot_a, tot_b], axis=1)
        t = r * R + lax.broadcasted_iota(jnp.int32, (R, 1), 0)
        cnt = (jnp.minimum(t + half, S) - jnp.maximum(t - half, 0)).astype(F32)
        d = tot / cnt - jnp.concatenate([x0, x1], axis=1)
        y = _dot(d.astype(BF16), w_ref[...]) * scale_ref[...]
        o_ref[0, r * R:(r + 1) * R, :] = y.astype(BF16)


def _pool(pool_in, lw):
    B, S, W = pool_in.shape
    return pl.pallas_call(
        functools.partial(_pool_kernel, S=S),
        out_shape=jax.ShapeDtypeStruct((B, S, W), BF16),
        grid=(B,),
        in_specs=[pl.BlockSpec((1, S, W), lambda b: (b, 0, 0)), _const_spec((W, W)), _const_spec((1, W))],
        out_specs=pl.BlockSpec((1, S, W), lambda b: (b, 0, 0)),
        scratch_shapes=[pltpu.VMEM((S + 2 * POOL_PAD, W), F32)],
        compiler_params=_params(("parallel",)),
        name="pool",
    )(pool_in, lw['w_pool'], lw['pool_scale'])


def _kv_expand_kernel(ckr_ref, wk_ref, wv_ref, kt_ref, v_ref):
    a = ckr_ref[0].astype(BF16)
    for hh in range(MLA_HEADS):
        kt = lax.dot_general(wk_ref[hh], a, (((1,), (1,)), ((), ())), preferred_element_type=F32)
        kt_ref[0, hh] = kt.astype(BF16)
    v_ref[0] = _dot(a[:, :KV_LORA], wv_ref[...]).astype(BF16)


def _kv_expand(ckr, lw, *, tt):
    B, T, W = ckr.shape
    return pl.pallas_call(
        _kv_expand_kernel,
        out_shape=[jax.ShapeDtypeStruct((B, MLA_HEADS, HEAD_PAD, T), BF16),
                   jax.ShapeDtypeStruct((B, T, MLA_HEADS * V_DIM), BF16)],
        grid=(B, T // tt),
        in_specs=[pl.BlockSpec((1, tt, W), lambda b, i: (b, i, 0)),
                  _const_spec((MLA_HEADS, HEAD_PAD, W)),
                  _const_spec((KV_LORA, MLA_HEADS * V_DIM))],
        out_specs=[pl.BlockSpec((1, MLA_HEADS, HEAD_PAD, tt), lambda b, i: (b, 0, 0, i)),
                   pl.BlockSpec((1, tt, MLA_HEADS * V_DIM), lambda b, i: (b, i, 0))],
        compiler_params=_params(("parallel", "parallel")),
        name="kv_expand",
    )(ckr, lw['wk_t'], lw['w_v'])


def _attention_kernel(q_ref, kt_ref, v_ref, o_ref):
    lane = lax.broadcasted_iota(jnp.int32, (1, LANES), 1)
    v = v_ref[0]
    outs = []
    for j in range(2):
        s = _dot(q_ref[0, :, j * HEAD_PAD:(j + 1) * HEAD_PAD], kt_ref[0, j])
        m = jnp.max(s, axis=-1, keepdims=True)
        p = jnp.exp(s - m)
        l = jnp.sum(p, axis=-1, keepdims=True)
        outs.append(_dot(p.astype(BF16), v) / l)
    o_ref[0] = jnp.where(lane < V_DIM, outs[0], outs[1]).astype(BF16)


def _attention(q, kt, v, *, tq):
    B, S, _ = q.shape
    T = v.shape[1]
    return pl.pallas_call(
        _attention_kernel,
        out_shape=jax.ShapeDtypeStruct((B, S, MLA_HEADS * V_DIM), BF16),
        grid=(B, MLA_HEADS // 2, S // tq),
        in_specs=[pl.BlockSpec((1, tq, 2 * HEAD_PAD), lambda b, hp, i: (b, i, hp)),
                  pl.BlockSpec((1, 2, HEAD_PAD, T), lambda b, hp, i: (b, hp, 0, 0)),
                  pl.BlockSpec((1, T, 2 * V_DIM), lambda b, hp, i: (b, 0, hp))],
        out_specs=pl.BlockSpec((1, tq, 2 * V_DIM), lambda b, hp, i: (b, i, hp)),
        compiler_params=_params(("parallel", "parallel", "parallel")),
        name="attention",
    )(q, kt, v)


def _post_mixer_kernel(x_ref, attn_ref, pool_ref, sgu_ref, mod_ref, wout_ref, gpost_ref, gffn_ref, x1_ref, h2_ref):
    mix = jnp.concatenate([attn_ref[0], pool_ref[0], sgu_ref[0]], axis=-1)
    out = _dot(mix, wout_ref[...])
    x1 = x_ref[0] + mod_ref[0, 2:3, :] * _rms(out, gpost_ref[...])
    x1_ref[0] = x1
    h2 = _rms(x1, gffn_ref[...]) * (1 + mod_ref[0, 4:5, :]) + mod_ref[0, 3:4, :]
    h2_ref[0] = h2.astype(BF16)


def _post_mixer(x, attn, pool, sgu, mod, lw, *, latent, tm):
    B, S, D = x.shape
    mod_idx = (lambda b, i: (b + 1, 0, 0)) if latent else (lambda b, i: (0, 0, 0))
    tok = lambda b, i: (b, i, 0)
    return pl.pallas_call(
        _post_mixer_kernel,
        out_shape=[jax.ShapeDtypeStruct((B, S, D), F32), jax.ShapeDtypeStruct((B, S, D), BF16)],
        grid=(B, S // tm),
        in_specs=[pl.BlockSpec((1, tm, D), tok),
                  pl.BlockSpec((1, tm, attn.shape[-1]), tok),
                  pl.BlockSpec((1, tm, POOL_W), tok),
                  pl.BlockSpec((1, tm, SGU_W), tok),
                  pl.BlockSpec((1, 8, D), mod_idx),
                  _const_spec(lw['w_out'].shape),
                  _const_spec((1, D)),
                  _const_spec((1, D))],
        out_specs=[pl.BlockSpec((1, tm, D), tok), pl.BlockSpec((1, tm, D), tok)],
        compiler_params=_params(("parallel", "parallel")),
        name="post_mixer",
    )(x, attn, pool, sgu, mod, lw['w_out'], lw['g_post_mix'], lw['g_pre_ffn'])


def _conv_ffn_kernel(h_ref, hprev_ref, hnext_ref, x1_ref, mod_ref, wup_ref, cw_ref, wdown_ref, gpost_ref, o_ref,
                     hext_ref, zg_ref, zv_ref, acc_ref, *, tm, n_chunks):
    i = pl.program_id(1)
    halo = BF16_SUBLANES
    zero = jnp.zeros((), BF16)
    hext_ref[0:halo, :] = jnp.where(i > 0, hprev_ref[0], zero)
    hext_ref[halo:halo + tm, :] = h_ref[0]
    hext_ref[halo + tm:halo + tm + halo, :] = jnp.where(i < pl.num_programs(1) - 1, hnext_ref[0], zero)
    acc_ref[...] = jnp.zeros_like(acc_ref)

    def conv(z_ref, cw):
        return (z_ref[halo - 1:halo - 1 + tm, :] * cw[0:1, :] + z_ref[halo:halo + tm, :] * cw[1:2, :]
                + z_ref[halo + 1:halo + 1 + tm, :] * cw[2:3, :] + cw[3:4, :])

    def body(c, carry):
        hext = hext_ref[...]
        zg_ref[...] = _dot(hext, wup_ref[c])
        zv_ref[...] = _dot(hext, wup_ref[n_chunks + c])
        g = conv(zg_ref, cw_ref[c])
        val = conv(zv_ref, cw_ref[n_chunks + c])
        a = (jax.nn.silu(g) * val).astype(BF16)
        acc_ref[...] += _dot(a, wdown_ref[c])
        return carry

    lax.fori_loop(0, n_chunks, body, 0)
    o_ref[0] = x1_ref[0] + mod_ref[0, 5:6, :] * _rms(acc_ref[...], gpost_ref[...])


def _conv_ffn(h2, x1, mod, lw, *, latent, tm):
    B, S, D = x1.shape
    n_chunks = lw['w_down'].shape[0]
    halo = BF16_SUBLANES
    nblk = tm // halo
    last = S // halo - 1
    mod_idx = (lambda b, i: (b + 1, 0, 0)) if latent else (lambda b, i: (0, 0, 0))
    tok = lambda b, i: (b, i, 0)
    return pl.pallas_call(
        functools.partial(_conv_ffn_kernel, tm=tm, n_chunks=n_chunks),
        out_shape=jax.ShapeDtypeStruct((B, S, D), F32),
        grid=(B, S // tm),
        in_specs=[pl.BlockSpec((1, tm, D), tok),
                  pl.BlockSpec((1, halo, D), lambda b, i: (b, jnp.maximum(i * nblk - 1, 0), 0)),
                  pl.BlockSpec((1, halo, D), lambda b, i: (b, jnp.minimum((i + 1) * nblk, last), 0)),
                  pl.BlockSpec((1, tm, D), tok),
                  pl.BlockSpec((1, 8, D), mod_idx),
                  _const_spec(lw['w_up'].shape),
                  _const_spec(lw['conv'].shape),
                  _const_spec(lw['w_down'].shape),
                  _const_spec((1, D))],
        out_specs=pl.BlockSpec((1, tm, D), tok),
        scratch_shapes=[pltpu.VMEM((tm + 2 * halo, D), BF16),
                        pltpu.VMEM((tm + 2 * halo, FF_CHUNK), F32),
                        pltpu.VMEM((tm + 2 * halo, FF_CHUNK), F32),
                        pltpu.VMEM((tm, D), F32)],
        compiler_params=_params(("parallel", "parallel")),
        name="conv_ffn",
    )(h2, h2, h2, x1, mod, lw['w_up'], lw['conv'], lw['w_down'], lw['g_post_ffn'])


def _rope_tables(S):
    n_rows = S // GRID_W
    rows = jnp.repeat(jnp.arange(n_rows, dtype=F32), GRID_W)
    cols = jnp.tile(jnp.arange(GRID_W, dtype=F32), n_rows)
    freqs = ROPE_BASE ** (-jnp.arange(ROPE_FREQS, dtype=F32) / ROPE_FREQS)
    ang = jnp.stack([rows[:, None] * freqs, cols[:, None] * freqs], axis=1)
    cos, sin = jnp.cos(ang), jnp.sin(ang)
    ck = jnp.stack([cos, cos], axis=2).reshape(S, ROPE_DIM)
    sk = jnp.stack([-sin, sin], axis=2).reshape(S, ROPE_DIM)
    pad = jnp.zeros((S, HEAD_PAD - NOPE_DIM - ROPE_DIM), F32)
    cq = jnp.concatenate([jnp.full((S, NOPE_DIM), SM_SCALE, F32), ck * SM_SCALE, pad], axis=1)
    sq = jnp.concatenate([jnp.zeros((S, NOPE_DIM), F32), sk * SM_SCALE, pad], axis=1)
    return cq, sq, ck, sk


def _swap_halves(w):
    shp = w.shape
    w4 = w.reshape(*shp[:-1], 2, 2, ROPE_FREQS)
    return w4[..., ::-1, :].reshape(shp)


def _layer_weights(l, g_pre_mix, g_post_mix, g_pre_ffn, g_post_ffn, w_in, g_q_a, w_q_b, g_kv_a, w_kv_b,
                   w_pool, pool_scale, g_sgu, w_sgu, b_sgu, w_out, w_up, conv_w, conv_b, w_down):
    D = w_in.shape[1]
    o1 = Q_LORA
    o2 = o1 + KV_LORA
    o3 = o2 + ROPE_DIM
    o4 = o3 + POOL_W
    wi = w_in[l]
    w_kr = wi[:, o2:o3]
    zpad = jnp.zeros((D, LANES - ROPE_DIM), F32)
    w_in_r = jnp.concatenate([wi[:, :o2], wi[:, o3:], w_kr, zpad, _swap_halves(w_kr), zpad], axis=1)

    qd = NOPE_DIM + ROPE_DIM
    wq = w_q_b[l].reshape(Q_LORA, MLA_HEADS, qd)
    hz = jnp.zeros((Q_LORA, MLA_HEADS, HEAD_PAD - qd), F32)
    wq_main = jnp.concatenate([wq, hz], axis=-1).reshape(Q_LORA, MLA_HEADS * HEAD_PAD)
    wq_sw = jnp.concatenate([jnp.zeros((Q_LORA, MLA_HEADS, NOPE_DIM), F32), _swap_halves(wq[..., NOPE_DIM:]), hz],
                            axis=-1).reshape(Q_LORA, MLA_HEADS * HEAD_PAD)

    wkv = w_kv_b[l].reshape(KV_LORA, MLA_HEADS, NOPE_DIM + V_DIM)
    wk_t = jnp.transpose(wkv[..., :NOPE_DIM], (1, 2, 0))
    wk_full = jnp.zeros((MLA_HEADS, HEAD_PAD, KV_LORA + ROPE_DIM), F32)
    wk_full = wk_full.at[:, :NOPE_DIM, :KV_LORA].set(wk_t)
    wk_full = wk_full.at[:, NOPE_DIM:qd, KV_LORA:].set(jnp.eye(ROPE_DIM, dtype=F32))
    w_v = wkv[..., NOPE_DIM:].reshape(KV_LORA, MLA_HEADS * V_DIM)

    wp = jnp.zeros((POOL_W, POOL_W), F32)
    for g in range(len(POOL_WINDOWS)):
        wp = wp.at[g * POOL_GC:(g + 1) * POOL_GC, g * POOL_GC:(g + 1) * POOL_GC].set(w_pool[l, g])

    d_ff = w_down.shape[1]
    nch = d_ff // FF_CHUNK
    wu = w_up[l].reshape(D, 2 * nch, FF_CHUNK).transpose(1, 0, 2)
    conv = jnp.concatenate([conv_w[l], conv_b[l][None, :], jnp.zeros((4, 2 * d_ff), F32)], axis=0)
    conv = conv.reshape(8, 2 * nch, FF_CHUNK).transpose(1, 0, 2)
    row = lambda a: a[l][None, :]
    return {
        'g_pre_mix': row(g_pre_mix), 'g_post_mix': row(g_post_mix),
        'g_pre_ffn': row(g_pre_ffn), 'g_post_ffn': row(g_post_ffn),
        'w_in': w_in_r.astype(BF16), 'g_q_a': row(g_q_a),
        'w_q_b': jnp.concatenate([wq_main, wq_sw], axis=1).astype(BF16),
        'g_kv_a': row(g_kv_a), 'wk_t': wk_full.astype(BF16), 'w_v': w_v.astype(BF16),
        'w_pool': wp.astype(BF16), 'pool_scale': row(pool_scale),
        'g_sgu': row(g_sgu), 'w_sgu': w_sgu[l].astype(BF16),
        'b_sgu': jnp.repeat(b_sgu[l].T, SGU_HD, axis=1),
        'w_out': w_out[l].astype(BF16), 'w_up': wu.astype(BF16), 'conv': conv,
        'w_down': w_down[l].reshape(nch, FF_CHUNK, D).astype(BF16),
    }


def _layer(x, mod, lw, tabs, ctx_ckr, *, latent, tm, tq, tt):
    q, ckv, kr, pool_in, sgu = _pre_mixer(x, mod, lw, tabs, latent=latent, tm=tm)
    ckr = jnp.concatenate([ckv, kr], axis=-1)
    if latent:
        ckr = jnp.concatenate([ctx_ckr, ckr], axis=1)
    kt, v = _kv_expand(ckr, lw, tt=tt)
    attn = _attention(q, kt, v, tq=tq)
    pool = _pool(pool_in, lw)
    x1, h2 = _post_mixer(x, attn, pool, sgu, mod, lw, latent=latent, tm=tm)
    return _conv_ffn(h2, x1, mod, lw, latent=latent, tm=tm), ckv, kr


def kernel(x_prompt, x_sample, cache_ckv, cache_krope, c, c_ctx, w_mod, b_mod, g_pre_mix, g_post_mix, g_pre_ffn, g_post_ffn, w_in, g_q_a, w_q_b, g_kv_a, w_kv_b, w_pool, pool_scale, g_sgu, w_sgu, b_sgu, w_out, w_up, conv_w, conv_b, w_down):
    depth = w_in.shape[0]
    D = x_prompt.shape[-1]
    n_lat = c.shape[0]
    mod_rows = 16
    cc = jnp.concatenate([c_ctx[None, :], c, jnp.zeros((mod_rows - 1 - n_lat, D), F32)], axis=0)
    mod_all = _modulation(cc, w_mod, b_mod)
    mod_all = jnp.pad(mod_all.reshape(depth, mod_rows, 6, D), ((0, 0), (0, 0), (0, 2), (0, 0)))
    tabs = _rope_tables(x_sample.shape[1])

    xp, xs = x_prompt, x_sample
    ckv_list, kr_list = [], []
    for l in range(depth):
        lw = _layer_weights(l, g_pre_mix, g_post_mix, g_pre_ffn, g_post_ffn, w_in, g_q_a, w_q_b, g_kv_a, w_kv_b,
                            w_pool, pool_scale, g_sgu, w_sgu, b_sgu, w_out, w_up, conv_w, conv_b, w_down)
        xp, ckv, kr = _layer(xp, mod_all[l], lw, None, None, latent=False, tm=256, tq=256, tt=256)
        ckv_list.append(ckv)
        kr_list.append(kr)
        ctx_ckr = jnp.concatenate([cache_ckv[:, l], cache_krope[:, l]], axis=-1)
        xs, _, _ = _layer(xs, mod_all[l], lw, tabs, ctx_ckr, latent=True, tm=512, tq=256, tt=512)
    return (xp, xs, jnp.stack(ckv_list, axis=1), jnp.stack(kr_list, axis=1))
```

```python
import functools

import jax
import jax.numpy as jnp
from jax import lax
from jax.experimental import pallas as pl
from jax.experimental.pallas import tpu as pltpu

F32 = jnp.float32
BF16 = jnp.bfloat16

GRID_W = 64
MLA_HEADS = 8
NOPE_DIM = 64
ROPE_DIM = 32
ROPE_FREQS = ROPE_DIM // 4
V_DIM = 64
Q_LORA = 256
KV_LORA = 128
ROPE_BASE = 10000.0
SM_SCALE = (NOPE_DIM + ROPE_DIM) ** -0.5
LOG2E = 1.4426950408889634
Q_SCALE = SM_SCALE * LOG2E
POOL_WINDOWS = (2, 4, 8, 16)
POOL_GC = 64
POOL_W = POOL_GC * len(POOL_WINDOWS)
SGU_HEADS = 4
SGU_HD = 64
SGU_W = SGU_HEADS * SGU_HD
CHUNK = 128
EPS = 1e-6

LANES = 128
BF16_SUBLANES = 16
HEAD_PAD = LANES
VMEM_LIMIT = 56 * 1024 * 1024

C_QA = 0
C_KVA = C_QA + Q_LORA
C_POOL = C_KVA + KV_LORA
C_U = C_POOL + POOL_W
C_V = C_U + SGU_W
C_KR = C_V + SGU_W
C_KRSW = C_KR + LANES
IN_COLS_PAD = C_KRSW + LANES

FF_CHUNK = 256
FF_GATE_ROWS = 64
MXU_DIM = 256
POOL_PAD = 16
POOL_ROWS = 256


def _rms(x, g):
    return x * lax.rsqrt(jnp.mean(x * x, axis=-1, keepdims=True) + EPS) * g


def _dot(a, b):
    return jnp.dot(a, b, preferred_element_type=F32)


def _params(sem):
    return pltpu.CompilerParams(dimension_semantics=sem, vmem_limit_bytes=VMEM_LIMIT)


def _const_spec(shape):
    nd = len(shape)
    return pl.BlockSpec(shape, lambda *_: (0,) * nd)


def _mod_kernel(c_ref, w_ref, b_ref, o_ref):
    a = jax.nn.silu(c_ref[...]).astype(BF16)
    o_ref[0] = _dot(a, w_ref[0].astype(BF16)) + b_ref[0]


def _modulation(cc, w_mod, b_mod):
    depth, d, e = w_mod.shape
    rows = cc.shape[0]
    te = 1536
    return pl.pallas_call(
        _mod_kernel,
        out_shape=jax.ShapeDtypeStruct((depth, rows, e), F32),
        grid=(depth, e // te),
        in_specs=[
            pl.BlockSpec((rows, d), lambda l, j: (0, 0)),
            pl.BlockSpec((1, d, te), lambda l, j: (l, 0, j)),
            pl.BlockSpec((1, 1, te), lambda l, j: (l, 0, j)),
        ],
        out_specs=pl.BlockSpec((1, rows, te), lambda l, j: (l, 0, j)),
        compiler_params=_params(("parallel", "parallel")),
        name="modulation",
    )(cc, w_mod, b_mod.reshape(depth, 1, e))


def _pre_mixer_kernel(*refs, latent, tm):
    if latent:
        (x_ref, mod_ref, gpre_ref, win_ref, gqa_ref, wqb_ref, gkva_ref, gsgu_ref, wsgu_ref, bsgu_ref,
         cq_ref, sq_ref, ck_ref, sk_ref, q_ref, ckv_ref, kr_ref, pool_ref, sgu_ref) = refs
    else:
        (x_ref, mod_ref, gpre_ref, win_ref, gqa_ref, wqb_ref, gkva_ref, gsgu_ref, wsgu_ref, bsgu_ref,
         q_ref, ckv_ref, kr_ref, pool_ref, sgu_ref) = refs
    x = x_ref[0]
    shift = mod_ref[0, 0:1, :]
    scale = mod_ref[0, 1:2, :]
    h = _rms(x, gpre_ref[...]) * (1 + scale) + shift
    proj = _dot(h.astype(BF16), win_ref[...])

    qn = _rms(proj[:, C_QA:C_QA + Q_LORA], gqa_ref[...]).astype(BF16)
    qq = _dot(qn, wqb_ref[...])
    width = MLA_HEADS * HEAD_PAD
    for hh in range(MLA_HEADS):
        lo = hh * HEAD_PAD
        if latent:
            qh = qq[:, lo:lo + HEAD_PAD] * cq_ref[...] + qq[:, width + lo:width + lo + HEAD_PAD] * sq_ref[...]
        else:
            qh = qq[:, lo:lo + HEAD_PAD] * Q_SCALE
        q_ref[0, :, lo:lo + HEAD_PAD] = qh.astype(BF16)

    ckv_ref[0] = _rms(proj[:, C_KVA:C_KVA + KV_LORA], gkva_ref[...])
    kr = proj[:, C_KR:C_KR + ROPE_DIM]
    if latent:
        kr = kr * ck_ref[...] + proj[:, C_KRSW:C_KRSW + ROPE_DIM] * sk_ref[...]
    kr_ref[0] = kr
    pool_ref[0] = proj[:, C_POOL:C_POOL + POOL_W]

    u = proj[:, C_U:C_U + SGU_W]
    vn = _rms(proj[:, C_V:C_V + SGU_W], gsgu_ref[...]).astype(BF16)
    lane = lax.broadcasted_iota(jnp.int32, (1, LANES), 1)
    zero = jnp.zeros((), BF16)
    for c in range(tm // CHUNK):
        r0 = c * CHUNK
        for pair in range(SGU_W // LANES):
            l0 = pair * LANES
            vp = vn[r0:r0 + CHUNK, l0:l0 + LANES]
            v_lo = jnp.where(lane < SGU_HD, vp, zero)
            v_hi = jnp.where(lane >= SGU_HD, vp, zero)
            z = _dot(wsgu_ref[2 * pair], v_lo) + _dot(wsgu_ref[2 * pair + 1], v_hi) + bsgu_ref[:, l0:l0 + LANES]
            sgu_ref[0, r0:r0 + CHUNK, l0:l0 + LANES] = (u[r0:r0 + CHUNK, l0:l0 + LANES] * z).astype(BF16)


def _pre_mixer(x, mod, lw, tabs, *, latent, tm):
    B, S, D = x.shape
    mod_idx = (lambda b, i: (b + 1, 0, 0)) if latent else (lambda b, i: (0, 0, 0))
    tok = lambda b, i: (b, i, 0)
    in_specs = [
        pl.BlockSpec((1, tm, D), tok),
        pl.BlockSpec((1, 8, D), mod_idx),
        _const_spec((1, D)),
        _const_spec(lw['w_in'].shape),
        _const_spec((1, Q_LORA)),
        _const_spec(lw['w_q_b'].shape if latent else (Q_LORA, MLA_HEADS * HEAD_PAD)),
        _const_spec((1, KV_LORA)),
        _const_spec((1, SGU_W)),
        _const_spec((SGU_HEADS, CHUNK, CHUNK)),
        _const_spec((CHUNK, SGU_W)),
    ]
    args = [x, mod, lw['g_pre_mix'], lw['w_in'], lw['g_q_a'], lw['w_q_b'], lw['g_kv_a'], lw['g_sgu'],
            lw['w_sgu'], lw['b_sgu']]
    if latent:
        seq = lambda b, i: (i, 0)
        in_specs += [pl.BlockSpec((tm, HEAD_PAD), seq), pl.BlockSpec((tm, HEAD_PAD), seq),
                     pl.BlockSpec((tm, ROPE_DIM), seq), pl.BlockSpec((tm, ROPE_DIM), seq)]
        args += list(tabs)
    out_shape = [
        jax.ShapeDtypeStruct((B, S, MLA_HEADS * HEAD_PAD), BF16),
        jax.ShapeDtypeStruct((B, S, KV_LORA), F32),
        jax.ShapeDtypeStruct((B, S, ROPE_DIM), F32),
        jax.ShapeDtypeStruct((B, S, POOL_W), F32),
        jax.ShapeDtypeStruct((B, S, SGU_W), BF16),
    ]
    out_specs = [
        pl.BlockSpec((1, tm, MLA_HEADS * HEAD_PAD), tok),
        pl.BlockSpec((1, tm, KV_LORA), tok),
        pl.BlockSpec((1, tm, ROPE_DIM), tok),
        pl.BlockSpec((1, tm, POOL_W), tok),
        pl.BlockSpec((1, tm, SGU_W), tok),
    ]
    return pl.pallas_call(
        functools.partial(_pre_mixer_kernel, latent=latent, tm=tm),
        out_shape=out_shape,
        grid=(B, S // tm),
        in_specs=in_specs,
        out_specs=out_specs,
        compiler_params=_params(("parallel", "parallel")),
        name="pre_mixer_lat" if latent else "pre_mixer_ctx",
    )(*args)


def _pool_kernel(x_ref, w_ref, scale_ref, o_ref, pad_ref, *, S):
    zeros = jnp.zeros((POOL_PAD, POOL_W), F32)
    pad_ref[0:POOL_PAD, :] = zeros
    pad_ref[POOL_PAD + S:POOL_PAD + S + POOL_PAD, :] = zeros
    pad_ref[POOL_PAD:POOL_PAD + S, :] = x_ref[0]

    R = min(POOL_ROWS, S)
    lane = lax.broadcasted_iota(jnp.int32, (1, LANES), 1)
    lane_w = lax.broadcasted_iota(jnp.int32, (1, POOL_W), 1)
    half = jnp.where(lane_w < POOL_GC, POOL_WINDOWS[0] // 2,
                     jnp.where(lane_w < 2 * POOL_GC, POOL_WINDOWS[1] // 2,
                               jnp.where(lane_w < 3 * POOL_GC, POOL_WINDOWS[2] // 2, POOL_WINDOWS[3] // 2)))
    for r in range(S // R):
        base = POOL_PAD + r * R

        def ld(off, lo):
            return pad_ref[base + off:base + off + R, lo:lo + LANES]

        x0 = ld(0, 0)
        s2 = ld(-1, 0) + x0
        s4 = s2 + ld(-2, 0) + ld(1, 0)
        tot_a = jnp.where(lane < POOL_GC, s2, s4)
        x1 = ld(0, LANES)
        s8 = x1
        for off in (-4, -3, -2, -1, 1, 2, 3):
            s8 = s8 + ld(off, LANES)
        s16 = s8
        for off in (-8, -7, -6, -5, 4, 5, 6, 7):
            s16 = s16 + ld(off, LANES)
        tot_b = jnp.where(lane < POOL_GC, s8, s16)
        tot = jnp.concatenate([tot_a, tot_b], axis=1)
        t = r * R + lax.broadcasted_iota(jnp.int32, (R, 1), 0)
        cnt = (jnp.minimum(t + half, S) - jnp.maximum(t - half, 0)).astype(F32)
        d = tot / cnt - jnp.concatenate([x0, x1], axis=1)
        y = _dot(d.astype(BF16), w_ref[...]) * scale_ref[...]
        o_ref[0, r * R:(r + 1) * R, :] = y.astype(BF16)


def _pool(pool_in, lw):
    B, S, W = pool_in.shape
    return pl.pallas_call(
        functools.partial(_pool_kernel, S=S),
        out_shape=jax.ShapeDtypeStruct((B, S, W), BF16),
        grid=(B,),
        in_specs=[pl.BlockSpec((1, S, W), lambda b: (b, 0, 0)), _const_spec((W, W)), _const_spec((1, W))],
        out_specs=pl.BlockSpec((1, S, W), lambda b: (b, 0, 0)),
        scratch_shapes=[pltpu.VMEM((S + 2 * POOL_PAD, W), F32)],
        compiler_params=_params(("parallel",)),
        name="pool",
    )(pool_in, lw['w_pool'], lw['pool_scale'])


def _kv_expand_kernel(ckr_ref, wk_ref, wv_ref, kt_ref, v_ref):
    a = ckr_ref[0].astype(BF16)
    for hh in range(MLA_HEADS):
        kt = lax.dot_general(wk_ref[hh], a, (((1,), (1,)), ((), ())), preferred_element_type=F32)
        kt_ref[0, hh] = kt.astype(BF16)
    v_ref[0] = _dot(a[:, :KV_LORA], wv_ref[...]).astype(BF16)


def _kv_expand(ckr, lw, *, tt):
    B, T, W = ckr.shape
    return pl.pallas_call(
        _kv_expand_kernel,
        out_shape=[jax.ShapeDtypeStruct((B, MLA_HEADS, HEAD_PAD, T), BF16),
                   jax.ShapeDtypeStruct((B, T, MLA_HEADS * V_DIM), BF16)],
        grid=(B, T // tt),
        in_specs=[pl.BlockSpec((1, tt, W), lambda b, i: (b, i, 0)),
                  _const_spec((MLA_HEADS, HEAD_PAD, W)),
                  _const_spec((KV_LORA, MLA_HEADS * V_DIM))],
        out_specs=[pl.BlockSpec((1, MLA_HEADS, HEAD_PAD, tt), lambda b, i: (b, 0, 0, i)),
                   pl.BlockSpec((1, tt, MLA_HEADS * V_DIM), lambda b, i: (b, i, 0))],
        compiler_params=_params(("parallel", "parallel")),
        name="kv_expand",
    )(ckr, lw['wk_t'], lw['w_v'])


def _attention_kernel(q_ref, kt_ref, v_ref, o_ref, s_ref, *, tq, T, tc):
    nc = T // tc
    lane = lax.broadcasted_iota(jnp.int32, (1, LANES), 1)

    def scores(h, c, mrun):
        s = _dot(q_ref[0, :, h * HEAD_PAD:(h + 1) * HEAD_PAD], kt_ref[0, h, :, c * tc:(c + 1) * tc])
        s_ref[h % 2, :, c * tc:(c + 1) * tc] = s
        for g in range(tc // LANES):
            mrun = jnp.maximum(mrun, s[:, g * LANES:(g + 1) * LANES])
        return mrun

    def probs_pv(h, c, m, lrun, acc):
        p = jnp.exp2(s_ref[h % 2, :, c * tc:(c + 1) * tc] - m)
        for g in range(tc // LANES):
            lrun = lrun + p[:, g * LANES:(g + 1) * LANES]
        hp = h // 2
        acc = acc + _dot(p.astype(BF16), v_ref[0, c * tc:(c + 1) * tc, hp * LANES:(hp + 1) * LANES])
        return lrun, acc

    m_prev = None
    pair_out = None
    for slot in range(MLA_HEADS + 1):
        mrun = jnp.full((tq, LANES), -jnp.inf, F32)
        lrun = jnp.zeros((tq, LANES), F32)
        acc = jnp.zeros((tq, LANES), F32)
        for c in range(nc):
            if slot < MLA_HEADS:
                mrun = scores(slot, c, mrun)
            if slot >= 1:
                lrun, acc = probs_pv(slot - 1, c, m_prev, lrun, acc)
        if slot >= 1:
            h = slot - 1
            o = acc / jnp.sum(lrun, axis=-1, keepdims=True)
            if h % 2 == 0:
                pair_out = o
            else:
                hp = h // 2
                o_ref[0, :, hp * LANES:(hp + 1) * LANES] = jnp.where(lane < V_DIM, pair_out, o).astype(BF16)
        if slot < MLA_HEADS:
            m_prev = jnp.max(mrun, axis=-1, keepdims=True)


def _attention(q, kt, v, *, tq, tc):
    B, S, _ = q.shape
    T = v.shape[1]
    return pl.pallas_call(
        functools.partial(_attention_kernel, tq=tq, T=T, tc=tc),
        out_shape=jax.ShapeDtypeStruct((B, S, MLA_HEADS * V_DIM), BF16),
        grid=(B, S // tq),
        in_specs=[pl.BlockSpec((1, tq, MLA_HEADS * HEAD_PAD), lambda b, i: (b, i, 0)),
                  pl.BlockSpec((1, MLA_HEADS, HEAD_PAD, T), lambda b, i: (b, 0, 0, 0)),
                  pl.BlockSpec((1, T, MLA_HEADS * V_DIM), lambda b, i: (b, 0, 0))],
        out_specs=pl.BlockSpec((1, tq, MLA_HEADS * V_DIM), lambda b, i: (b, i, 0)),
        scratch_shapes=[pltpu.VMEM((2, tq, T), F32)],
        compiler_params=_params(("parallel", "parallel")),
        name="attention",
    )(q, kt, v)


def _post_mixer_kernel(x_ref, attn_ref, pool_ref, sgu_ref, mod_ref, wout_ref, gpost_ref, gffn_ref, x1_ref, h2_ref):
    mix = jnp.concatenate([attn_ref[0], pool_ref[0], sgu_ref[0]], axis=-1)
    out = _dot(mix, wout_ref[...])
    x1 = x_ref[0] + mod_ref[0, 2:3, :] * _rms(out, gpost_ref[...])
    x1_ref[0] = x1
    h2 = _rms(x1, gffn_ref[...]) * (1 + mod_ref[0, 4:5, :]) + mod_ref[0, 3:4, :]
    h2_ref[0] = h2.astype(BF16)


def _post_mixer(x, attn, pool, sgu, mod, lw, *, latent, tm):
    B, S, D = x.shape
    mod_idx = (lambda b, i: (b + 1, 0, 0)) if latent else (lambda b, i: (0, 0, 0))
    tok = lambda b, i: (b, i, 0)
    return pl.pallas_call(
        _post_mixer_kernel,
        out_shape=[jax.ShapeDtypeStruct((B, S, D), F32), jax.ShapeDtypeStruct((B, S, D), BF16)],
        grid=(B, S // tm),
        in_specs=[pl.BlockSpec((1, tm, D), tok),
                  pl.BlockSpec((1, tm, attn.shape[-1]), tok),
                  pl.BlockSpec((1, tm, POOL_W), tok),
                  pl.BlockSpec((1, tm, SGU_W), tok),
                  pl.BlockSpec((1, 8, D), mod_idx),
                  _const_spec(lw['w_out'].shape),
                  _const_spec((1, D)),
                  _const_spec((1, D))],
        out_specs=[pl.BlockSpec((1, tm, D), tok), pl.BlockSpec((1, tm, D), tok)],
        compiler_params=_params(("parallel", "parallel")),
        name="post_mixer",
    )(x, attn, pool, sgu, mod, lw['w_out'], lw['g_post_mix'], lw['g_pre_ffn'])


def _interleave(a, b):
    out, ia, ib = [], 0, 0
    while ia < len(a) or ib < len(b):
        if ib >= len(b) or (ia < len(a) and ia * len(b) <= ib * len(a)):
            out.append(a[ia])
            ia += 1
        else:
            out.append(b[ib])
            ib += 1
    return out


def _conv_ffn_kernel(h_ref, hprev_ref, hnext_ref, x1_ref, mod_ref, wup_ref, cw_ref, wdown_ref, gpost_ref, o_ref,
                     hext_ref, zg0_ref, zv0_ref, zg1_ref, zv1_ref, a0_ref, a1_ref, acc_ref, *, tm, n_chunks, rb):
    i = pl.program_id(1)
    halo = BF16_SUBLANES
    D = hext_ref.shape[1]
    zero = jnp.zeros((), BF16)
    hext_ref[0:halo, :] = jnp.where(i > 0, hprev_ref[0], zero)
    hext_ref[halo:halo + tm, :] = h_ref[0]
    hext_ref[halo + tm:halo + tm + halo, :] = jnp.where(i < pl.num_programs(1) - 1, hnext_ref[0], zero)
    acc_ref[...] = jnp.zeros_like(acc_ref)
    zs = ((zg0_ref, zv0_ref), (zg1_ref, zv1_ref))
    as_ = (a0_ref, a1_ref)

    def conv_pair(z_ref, slab, cw, r0):
        n = rb // 2
        base = halo + r0
        e_m = z_ref[slab, pl.ds(base - 1, n, stride=2), :]
        e_c = z_ref[slab, pl.ds(base, n, stride=2), :]
        e_p = z_ref[slab, pl.ds(base + 1, n, stride=2), :]
        e_pp = z_ref[slab, pl.ds(base + 2, n, stride=2), :]
        w0, w1, w2, b = cw[0:1, :], cw[1:2, :], cw[2:3, :], cw[3:4, :]
        even = e_m * w0 + e_c * w1 + e_p * w2 + b
        odd = e_c * w0 + e_p * w1 + e_pp * w2 + b
        return even, odd

    def up_pieces(c, z_pair):
        pieces = []
        for which in range(2):
            state = {}

            def piece(k, which=which, state=state):
                part = _dot(hext_ref[:, k * MXU_DIM:(k + 1) * MXU_DIM],
                            wup_ref[which * n_chunks + c, k * MXU_DIM:(k + 1) * MXU_DIM, :])
                state['z'] = part if k == 0 else state['z'] + part
                if k == D // MXU_DIM - 1:
                    for slab in range(FF_CHUNK // LANES):
                        z_pair[which][slab] = state['z'][:, slab * LANES:(slab + 1) * LANES]

            pieces += [functools.partial(piece, k) for k in range(D // MXU_DIM)]
        return pieces

    def gate_pieces(c, z_pair, a_ref):
        def piece(r0):
            cwg = cw_ref[c]
            cwv = cw_ref[n_chunks + c]
            for slab in range(FF_CHUNK // LANES):
                ls = slice(slab * LANES, (slab + 1) * LANES)
                g_e, g_o = conv_pair(z_pair[0], slab, cwg[:, ls], r0)
                v_e, v_o = conv_pair(z_pair[1], slab, cwv[:, ls], r0)
                a_ref[r0 // 2:(r0 + rb) // 2, ls] = pltpu.pack_elementwise(
                    [jax.nn.silu(g_e) * v_e, jax.nn.silu(g_o) * v_o], packed_dtype=BF16)

        return [functools.partial(piece, r * rb) for r in range(tm // rb)]

    def down_pieces(c, a_ref):
        def piece(n):
            ns = slice(n * MXU_DIM, (n + 1) * MXU_DIM)
            acc_ref[:, ns] += _dot(pltpu.bitcast(a_ref[...], BF16), wdown_ref[c, :, ns])

        return [functools.partial(piece, n) for n in range(D // MXU_DIM)]

    for s in range(n_chunks + 2):
        par = s % 2
        mxu, vpu = [], []
        if s < n_chunks:
            mxu += up_pieces(s, zs[par])
        if 1 <= s <= n_chunks:
            vpu += gate_pieces(s - 1, zs[1 - par], as_[1 - par])
        if s >= 2:
            mxu += down_pieces(s - 2, as_[par])
        for emit in _interleave(mxu, vpu):
            emit()
    o_ref[0] = x1_ref[0] + mod_ref[0, 5:6, :] * _rms(acc_ref[...], gpost_ref[...])


def _conv_ffn(h2, x1, mod, lw, *, latent, tm):
    B, S, D = x1.shape
    n_chunks = lw['w_down'].shape[0]
    halo = BF16_SUBLANES
    nblk = tm // halo
    last = S // halo - 1
    mod_idx = (lambda b, i: (b + 1, 0, 0)) if latent else (lambda b, i: (0, 0, 0))
    tok = lambda b, i: (b, i, 0)
    z_scratch = pltpu.VMEM((FF_CHUNK // LANES, tm + 2 * halo, LANES), F32)
    a_scratch = pltpu.VMEM((tm // 2, FF_CHUNK), jnp.uint32)
    return pl.pallas_call(
        functools.partial(_conv_ffn_kernel, tm=tm, n_chunks=n_chunks, rb=FF_GATE_ROWS),
        out_shape=jax.ShapeDtypeStruct((B, S, D), F32),
        grid=(B, S // tm),
        in_specs=[pl.BlockSpec((1, tm, D), tok),
                  pl.BlockSpec((1, halo, D), lambda b, i: (b, jnp.maximum(i * nblk - 1, 0), 0)),
                  pl.BlockSpec((1, halo, D), lambda b, i: (b, jnp.minimum((i + 1) * nblk, last), 0)),
                  pl.BlockSpec((1, tm, D), tok),
                  pl.BlockSpec((1, 8, D), mod_idx),
                  _const_spec(lw['w_up'].shape),
                  _const_spec(lw['conv'].shape),
                  _const_spec(lw['w_down'].shape),
                  _const_spec((1, D))],
        out_specs=pl.BlockSpec((1, tm, D), tok),
        scratch_shapes=[pltpu.VMEM((tm + 2 * halo, D), BF16), z_scratch, z_scratch, z_scratch, z_scratch,
                        a_scratch, a_scratch, pltpu.VMEM((tm, D), F32)],
        compiler_params=_params(("parallel", "parallel")),
        name="conv_ffn",
    )(h2, h2, h2, x1, mod, lw['w_up'], lw['conv'], lw['w_down'], lw['g_post_ffn'])


def _rope_tables(S):
    n_rows = S // GRID_W
    rows = jnp.repeat(jnp.arange(n_rows, dtype=F32), GRID_W)
    cols = jnp.tile(jnp.arange(GRID_W, dtype=F32), n_rows)
    freqs = ROPE_BASE ** (-jnp.arange(ROPE_FREQS, dtype=F32) / ROPE_FREQS)
    ang = jnp.stack([rows[:, None] * freqs, cols[:, None] * freqs], axis=1)
    cos, sin = jnp.cos(ang), jnp.sin(ang)
    ck = jnp.stack([cos, cos], axis=2).reshape(S, ROPE_DIM)
    sk = jnp.stack([-sin, sin], axis=2).reshape(S, ROPE_DIM)
    pad = jnp.zeros((S, HEAD_PAD - NOPE_DIM - ROPE_DIM), F32)
    cq = jnp.concatenate([jnp.full((S, NOPE_DIM), Q_SCALE, F32), ck * Q_SCALE, pad], axis=1)
    sq = jnp.concatenate([jnp.zeros((S, NOPE_DIM), F32), sk * Q_SCALE, pad], axis=1)
    return cq, sq, ck, sk


def _swap_halves(w):
    shp = w.shape
    w4 = w.reshape(*shp[:-1], 2, 2, ROPE_FREQS)
    return w4[..., ::-1, :].reshape(shp)


def _layer_weights(l, g_pre_mix, g_post_mix, g_pre_ffn, g_post_ffn, w_in, g_q_a, w_q_b, g_kv_a, w_kv_b,
                   w_pool, pool_scale, g_sgu, w_sgu, b_sgu, w_out, w_up, conv_w, conv_b, w_down):
    D = w_in.shape[1]
    o1 = Q_LORA
    o2 = o1 + KV_LORA
    o3 = o2 + ROPE_DIM
    o4 = o3 + POOL_W
    wi = w_in[l]
    w_kr = wi[:, o2:o3]
    zpad = jnp.zeros((D, LANES - ROPE_DIM), F32)
    w_in_r = jnp.concatenate([wi[:, :o2], wi[:, o3:], w_kr, zpad, _swap_halves(w_kr), zpad], axis=1)

    qd = NOPE_DIM + ROPE_DIM
    wq = w_q_b[l].reshape(Q_LORA, MLA_HEADS, qd)
    hz = jnp.zeros((Q_LORA, MLA_HEADS, HEAD_PAD - qd), F32)
    wq_main = jnp.concatenate([wq, hz], axis=-1).reshape(Q_LORA, MLA_HEADS * HEAD_PAD)
    wq_sw = jnp.concatenate([jnp.zeros((Q_LORA, MLA_HEADS, NOPE_DIM), F32), _swap_halves(wq[..., NOPE_DIM:]), hz],
                            axis=-1).reshape(Q_LORA, MLA_HEADS * HEAD_PAD)

    wkv = w_kv_b[l].reshape(KV_LORA, MLA_HEADS, NOPE_DIM + V_DIM)
    wk_t = jnp.transpose(wkv[..., :NOPE_DIM], (1, 2, 0))
    wk_full = jnp.zeros((MLA_HEADS, HEAD_PAD, KV_LORA + ROPE_DIM), F32)
    wk_full = wk_full.at[:, :NOPE_DIM, :KV_LORA].set(wk_t)
    wk_full = wk_full.at[:, NOPE_DIM:qd, KV_LORA:].set(jnp.eye(ROPE_DIM, dtype=F32))
    w_v = wkv[..., NOPE_DIM:].reshape(KV_LORA, MLA_HEADS * V_DIM)

    wp = jnp.zeros((POOL_W, POOL_W), F32)
    for g in range(len(POOL_WINDOWS)):
        wp = wp.at[g * POOL_GC:(g + 1) * POOL_GC, g * POOL_GC:(g + 1) * POOL_GC].set(w_pool[l, g])

    d_ff = w_down.shape[1]
    nch = d_ff // FF_CHUNK
    wu = w_up[l].reshape(D, 2 * nch, FF_CHUNK).transpose(1, 0, 2)
    conv = jnp.concatenate([conv_w[l], conv_b[l][None, :], jnp.zeros((4, 2 * d_ff), F32)], axis=0)
    conv = conv.reshape(8, 2 * nch, FF_CHUNK).transpose(1, 0, 2)
    row = lambda a: a[l][None, :]
    return {
        'g_pre_mix': row(g_pre_mix), 'g_post_mix': row(g_post_mix),
        'g_pre_ffn': row(g_pre_ffn), 'g_post_ffn': row(g_post_ffn),
        'w_in': w_in_r.astype(BF16), 'g_q_a': row(g_q_a),
        'w_q_b': jnp.concatenate([wq_main, wq_sw], axis=1).astype(BF16),
        'g_kv_a': row(g_kv_a), 'wk_t': wk_full.astype(BF16), 'w_v': w_v.astype(BF16),
        'w_pool': wp.astype(BF16), 'pool_scale': row(pool_scale),
        'g_sgu': row(g_sgu), 'w_sgu': w_sgu[l].astype(BF16),
        'b_sgu': jnp.repeat(b_sgu[l].T, SGU_HD, axis=1),
        'w_out': w_out[l].astype(BF16), 'w_up': wu.astype(BF16), 'conv': conv,
        'w_down': w_down[l].reshape(nch, FF_CHUNK, D).astype(BF16),
    }


def _layer(x, mod, lw, tabs, ctx_ckr, *, latent, tm, tq, tt):
    q, ckv, kr, pool_in, sgu = _pre_mixer(x, mod, lw, tabs, latent=latent, tm=tm)
    ckr = jnp.concatenate([ckv, kr], axis=-1)
    if latent:
        ckr = jnp.concatenate([ctx_ckr, ckr], axis=1)
    kt, v = _kv_expand(ckr, lw, tt=tt)
    attn = _attention(q, kt, v, tq=tq, tc=min(512, kt.shape[-1]))
    pool = _pool(pool_in, lw)
    x1, h2 = _post_mixer(x, attn, pool, sgu, mod, lw, latent=latent, tm=tm)
    return _conv_ffn(h2, x1, mod, lw, latent=latent, tm=tm), ckv, kr


def kernel(x_prompt, x_sample, cache_ckv, cache_krope, c, c_ctx, w_mod, b_mod, g_pre_mix, g_post_mix, g_pre_ffn, g_post_ffn, w_in, g_q_a, w_q_b, g_kv_a, w_kv_b, w_pool, pool_scale, g_sgu, w_sgu, b_sgu, w_out, w_up, conv_w, conv_b, w_down):
    depth = w_in.shape[0]
    D = x_prompt.shape[-1]
    n_lat = c.shape[0]
    mod_rows = 16
    cc = jnp.concatenate([c_ctx[None, :], c, jnp.zeros((mod_rows - 1 - n_lat, D), F32)], axis=0)
    mod_all = _modulation(cc, w_mod, b_mod)
    mod_all = jnp.pad(mod_all.reshape(depth, mod_rows, 6, D), ((0, 0), (0, 0), (0, 2), (0, 0)))
    tabs = _rope_tables(x_sample.shape[1])

    xp, xs = x_prompt, x_sample
    ckv_list, kr_list = [], []
    for l in range(depth):
        lw = _layer_weights(l, g_pre_mix, g_post_mix, g_pre_ffn, g_post_ffn, w_in, g_q_a, w_q_b, g_kv_a, w_kv_b,
                            w_pool, pool_scale, g_sgu, w_sgu, b_sgu, w_out, w_up, conv_w, conv_b, w_down)
        xp, ckv, kr = _layer(xp, mod_all[l], lw, None, None, latent=False, tm=256, tq=256, tt=256)
        ckv_list.append(ckv)
        kr_list.append(kr)
        ctx_ckr = jnp.concatenate([cache_ckv[:, l], cache_krope[:, l]], axis=-1)
        xs, _, _ = _layer(xs, mod_all[l], lw, tabs, ctx_ckr, latent=True, tm=512, tq=256, tt=512)
    return (xp, xs, jnp.stack(ckv_list, axis=1), jnp.stack(kr_list, axis=1))
```

```python
import functools

import jax
import jax.numpy as jnp
from jax import lax
from jax.experimental import pallas as pl
from jax.experimental.pallas import tpu as pltpu

F32 = jnp.float32
BF16 = jnp.bfloat16

GRID_W = 64
MLA_HEADS = 8
NOPE_DIM = 64
ROPE_DIM = 32
ROPE_FREQS = ROPE_DIM // 4
V_DIM = 64
Q_LORA = 256
KV_LORA = 128
ROPE_BASE = 10000.0
SM_SCALE = (NOPE_DIM + ROPE_DIM) ** -0.5
LOG2E = 1.4426950408889634
Q_SCALE = SM_SCALE * LOG2E
POOL_WINDOWS = (2, 4, 8, 16)
POOL_GC = 64
POOL_W = POOL_GC * len(POOL_WINDOWS)
SGU_HEADS = 4
SGU_HD = 64
SGU_W = SGU_HEADS * SGU_HD
CHUNK = 128
EPS = 1e-6

LANES = 128
BF16_SUBLANES = 16
HEAD_PAD = LANES
VMEM_LIMIT = 56 * 1024 * 1024

C_QA = 0
C_KVA = C_QA + Q_LORA
C_POOL = C_KVA + KV_LORA
C_U = C_POOL + POOL_W
C_V = C_U + SGU_W
C_KR = C_V + SGU_W
C_KRSW = C_KR + LANES
IN_COLS_PAD = C_KRSW + LANES

FF_CHUNK = 256
FF_GATE_ROWS = 64
MXU_DIM = 256
POOL_PAD = 16
POOL_ROWS = 256


def _rms(x, g):
    return x * lax.rsqrt(jnp.mean(x * x, axis=-1, keepdims=True) + EPS) * g


def _dot(a, b):
    return jnp.dot(a, b, preferred_element_type=F32)


def _params(sem):
    return pltpu.CompilerParams(dimension_semantics=sem, vmem_limit_bytes=VMEM_LIMIT)


def _const_spec(shape):
    nd = len(shape)
    return pl.BlockSpec(shape, lambda *_: (0,) * nd)


def _mod_kernel(c_ref, w_ref, b_ref, o_ref):
    a = jax.nn.silu(c_ref[...]).astype(BF16)
    o_ref[0] = _dot(a, w_ref[0].astype(BF16)) + b_ref[0]


def _modulation(cc, w_mod, b_mod):
    depth, d, e = w_mod.shape
    rows = cc.shape[0]
    te = 1536
    return pl.pallas_call(
        _mod_kernel,
        out_shape=jax.ShapeDtypeStruct((depth, rows, e), F32),
        grid=(depth, e // te),
        in_specs=[
            pl.BlockSpec((rows, d), lambda l, j: (0, 0)),
            pl.BlockSpec((1, d, te), lambda l, j: (l, 0, j)),
            pl.BlockSpec((1, 1, te), lambda l, j: (l, 0, j)),
        ],
        out_specs=pl.BlockSpec((1, rows, te), lambda l, j: (l, 0, j)),
        compiler_params=_params(("parallel", "parallel")),
        name="modulation",
    )(cc, w_mod, b_mod.reshape(depth, 1, e))


def _pre_mixer_kernel(*refs, latent, tm):
    if latent:
        (x_ref, mod_ref, gpre_ref, win_ref, gqa_ref, wqb_ref, gkva_ref, gsgu_ref, wsgu_ref, bsgu_ref,
         cq_ref, sq_ref, ck_ref, sk_ref, q_ref, ckr_ref, pool_ref, sgu_ref) = refs
    else:
        (x_ref, mod_ref, gpre_ref, win_ref, gqa_ref, wqb_ref, gkva_ref, gsgu_ref, wsgu_ref, bsgu_ref,
         q_ref, ckr_ref, pool_ref, sgu_ref, ckv_ref, kr_ref) = refs
    x = x_ref[0]
    shift = mod_ref[0, 0:1, :]
    scale = mod_ref[0, 1:2, :]
    h = _rms(x, gpre_ref[...]) * (1 + scale) + shift
    proj = _dot(h.astype(BF16), win_ref[...])

    qn = _rms(proj[:, C_QA:C_QA + Q_LORA], gqa_ref[...]).astype(BF16)
    qq = _dot(qn, wqb_ref[...])
    width = MLA_HEADS * HEAD_PAD
    for hh in range(MLA_HEADS):
        lo = hh * HEAD_PAD
        if latent:
            qh = qq[:, lo:lo + HEAD_PAD] * cq_ref[...] + qq[:, width + lo:width + lo + HEAD_PAD] * sq_ref[...]
        else:
            qh = qq[:, lo:lo + HEAD_PAD] * Q_SCALE
        q_ref[0, :, lo:lo + HEAD_PAD] = qh.astype(BF16)

    ckv = _rms(proj[:, C_KVA:C_KVA + KV_LORA], gkva_ref[...])
    kr = proj[:, C_KR:C_KR + ROPE_DIM]
    if latent:
        kr = kr * ck_ref[...] + proj[:, C_KRSW:C_KRSW + ROPE_DIM] * sk_ref[...]
    else:
        ckv_ref[0] = ckv
        kr_ref[0] = kr
    ckr_ref[0, :, 0:KV_LORA] = ckv.astype(BF16)
    ckr_ref[0, :, KV_LORA:KV_LORA + ROPE_DIM] = kr.astype(BF16)
    pool_ref[0] = proj[:, C_POOL:C_POOL + POOL_W]

    u = proj[:, C_U:C_U + SGU_W]
    vn = _rms(proj[:, C_V:C_V + SGU_W], gsgu_ref[...]).astype(BF16)
    lane = lax.broadcasted_iota(jnp.int32, (1, LANES), 1)
    zero = jnp.zeros((), BF16)
    for c in range(tm // CHUNK):
        r0 = c * CHUNK
        for pair in range(SGU_W // LANES):
            l0 = pair * LANES
            vp = vn[r0:r0 + CHUNK, l0:l0 + LANES]
            v_lo = jnp.where(lane < SGU_HD, vp, zero)
            v_hi = jnp.where(lane >= SGU_HD, vp, zero)
            z = _dot(wsgu_ref[2 * pair], v_lo) + _dot(wsgu_ref[2 * pair + 1], v_hi) + bsgu_ref[:, l0:l0 + LANES]
            sgu_ref[0, r0:r0 + CHUNK, l0:l0 + LANES] = (u[r0:r0 + CHUNK, l0:l0 + LANES] * z).astype(BF16)


def _pre_mixer(x, mod, lw, tabs, *, latent, tm):
    B, S, D = x.shape
    mod_idx = (lambda b, i: (b + 1, 0, 0)) if latent else (lambda b, i: (0, 0, 0))
    tok = lambda b, i: (b, i, 0)
    in_specs = [
        pl.BlockSpec((1, tm, D), tok),
        pl.BlockSpec((1, 8, D), mod_idx),
        _const_spec((1, D)),
        _const_spec(lw['w_in'].shape),
        _const_spec((1, Q_LORA)),
        _const_spec(lw['w_q_b'].shape if latent else (Q_LORA, MLA_HEADS * HEAD_PAD)),
        _const_spec((1, KV_LORA)),
        _const_spec((1, SGU_W)),
        _const_spec((SGU_HEADS, CHUNK, CHUNK)),
        _const_spec((CHUNK, SGU_W)),
    ]
    args = [x, mod, lw['g_pre_mix'], lw['w_in'], lw['g_q_a'], lw['w_q_b'], lw['g_kv_a'], lw['g_sgu'],
            lw['w_sgu'], lw['b_sgu']]
    if latent:
        seq = lambda b, i: (i, 0)
        in_specs += [pl.BlockSpec((tm, HEAD_PAD), seq), pl.BlockSpec((tm, HEAD_PAD), seq),
                     pl.BlockSpec((tm, ROPE_DIM), seq), pl.BlockSpec((tm, ROPE_DIM), seq)]
        args += list(tabs)
    out_shape = [
        jax.ShapeDtypeStruct((B, S, MLA_HEADS * HEAD_PAD), BF16),
        jax.ShapeDtypeStruct((B, S, KV_LORA + ROPE_DIM), BF16),
        jax.ShapeDtypeStruct((B, S, POOL_W), F32),
        jax.ShapeDtypeStruct((B, S, SGU_W), BF16),
    ]
    out_specs = [
        pl.BlockSpec((1, tm, MLA_HEADS * HEAD_PAD), tok),
        pl.BlockSpec((1, tm, KV_LORA + ROPE_DIM), tok),
        pl.BlockSpec((1, tm, POOL_W), tok),
        pl.BlockSpec((1, tm, SGU_W), tok),
    ]
    if not latent:
        out_shape += [jax.ShapeDtypeStruct((B, S, KV_LORA), F32), jax.ShapeDtypeStruct((B, S, ROPE_DIM), F32)]
        out_specs += [pl.BlockSpec((1, tm, KV_LORA), tok), pl.BlockSpec((1, tm, ROPE_DIM), tok)]
    return pl.pallas_call(
        functools.partial(_pre_mixer_kernel, latent=latent, tm=tm),
        out_shape=out_shape,
        grid=(B, S // tm),
        in_specs=in_specs,
        out_specs=out_specs,
        compiler_params=_params(("parallel", "parallel")),
        name="pre_mixer_lat" if latent else "pre_mixer_ctx",
    )(*args)


def _pool_kernel(x_ref, w_ref, scale_ref, o_ref, pad_ref, *, S):
    nslab = POOL_W // LANES
    zeros = jnp.zeros((POOL_PAD, LANES), F32)
    for slab in range(nslab):
        pad_ref[slab, 0:POOL_PAD, :] = zeros
        pad_ref[slab, POOL_PAD + S:POOL_PAD + S + POOL_PAD, :] = zeros
        pad_ref[slab, POOL_PAD:POOL_PAD + S, :] = x_ref[0, :, slab * LANES:(slab + 1) * LANES]

    R = min(POOL_ROWS, S)
    n = R // 2
    lane = lax.broadcasted_iota(jnp.int32, (1, LANES), 1)
    lane_w = lax.broadcasted_iota(jnp.int32, (1, POOL_W), 1)
    half = jnp.where(lane_w < POOL_GC, POOL_WINDOWS[0] // 2,
                     jnp.where(lane_w < 2 * POOL_GC, POOL_WINDOWS[1] // 2,
                               jnp.where(lane_w < 3 * POOL_GC, POOL_WINDOWS[2] // 2, POOL_WINDOWS[3] // 2)))
    for r in range(S // R):
        base = POOL_PAD + r * R

        def window_sums(slab, lo_small, lo_big, par):
            def ld(off):
                return pad_ref[slab, pl.ds(base + par + off, n, stride=2), :]

            x = ld(0)
            small = x
            for off in range(-lo_small, lo_small):
                if off != 0:
                    small = small + ld(off)
            big = small
            for off in list(range(-lo_big, -lo_small)) + list(range(lo_small, lo_big)):
                big = big + ld(off)
            return x, jnp.where(lane < POOL_GC, small, big)

        ds = []
        for par in range(2):
            xa, tot_a = window_sums(0, POOL_WINDOWS[0] // 2, POOL_WINDOWS[1] // 2, par)
            xb, tot_b = window_sums(1, POOL_WINDOWS[2] // 2, POOL_WINDOWS[3] // 2, par)
            t = r * R + par + 2 * lax.broadcasted_iota(jnp.int32, (n, 1), 0)
            cnt = (jnp.minimum(t + half, S) - jnp.maximum(t - half, 0)).astype(F32)
            ds.append(jnp.concatenate([tot_a, tot_b], axis=1) / cnt - jnp.concatenate([xa, xb], axis=1))
        d = pltpu.bitcast(pltpu.pack_elementwise(ds, packed_dtype=BF16), BF16)
        y = _dot(d, w_ref[...]) * scale_ref[...]
        o_ref[0, r * R:(r + 1) * R, :] = y.astype(BF16)


def _pool(pool_in, lw):
    B, S, W = pool_in.shape
    return pl.pallas_call(
        functools.partial(_pool_kernel, S=S),
        out_shape=jax.ShapeDtypeStruct((B, S, W), BF16),
        grid=(B,),
        in_specs=[pl.BlockSpec((1, S, W), lambda b: (b, 0, 0)), _const_spec((W, W)), _const_spec((1, W))],
        out_specs=pl.BlockSpec((1, S, W), lambda b: (b, 0, 0)),
        scratch_shapes=[pltpu.VMEM((W // LANES, S + 2 * POOL_PAD, LANES), F32)],
        compiler_params=_params(("parallel",)),
        name="pool",
    )(pool_in, lw['w_pool'], lw['pool_scale'])


def _attention_kernel(q_ref, ckr_ref, wk_ref, wv_ref, o_ref, kt_ref, v_ref, s_ref, *, tq, T, tc):
    nc = T // tc
    lane = lax.broadcasted_iota(jnp.int32, (1, LANES), 1)

    @pl.when(pl.program_id(1) == 0)
    def _():
        for c in range(nc):
            a = ckr_ref[0, c * tc:(c + 1) * tc, :]
            kt = lax.dot_general(wk_ref[...], a, (((1,), (1,)), ((), ())), preferred_element_type=F32)
            kt_ref[:, c * tc:(c + 1) * tc] = kt.astype(BF16)
            v_ref[c * tc:(c + 1) * tc, :] = _dot(a[:, :KV_LORA], wv_ref[...]).astype(BF16)

    def scores(h, c, mrun):
        s = _dot(q_ref[0, :, h * HEAD_PAD:(h + 1) * HEAD_PAD],
                 kt_ref[h * HEAD_PAD:(h + 1) * HEAD_PAD, c * tc:(c + 1) * tc])
        s_ref[h % 2, :, c * tc:(c + 1) * tc] = s
        for g in range(tc // LANES):
            mrun = jnp.maximum(mrun, s[:, g * LANES:(g + 1) * LANES])
        return mrun

    def probs_pv(h, c, m, lrun, acc):
        p = jnp.exp2(s_ref[h % 2, :, c * tc:(c + 1) * tc] - m)
        for g in range(tc // LANES):
            lrun = lrun + p[:, g * LANES:(g + 1) * LANES]
        hp = h // 2
        acc = acc + _dot(p.astype(BF16), v_ref[c * tc:(c + 1) * tc, hp * LANES:(hp + 1) * LANES])
        return lrun, acc

    m_prev = None
    pair_out = None
    for slot in range(MLA_HEADS + 1):
        mrun = jnp.full((tq, LANES), -jnp.inf, F32)
        lrun = jnp.zeros((tq, LANES), F32)
        acc = jnp.zeros((tq, LANES), F32)
        for c in range(nc):
            if slot < MLA_HEADS:
                mrun = scores(slot, c, mrun)
            if slot >= 1:
                lrun, acc = probs_pv(slot - 1, c, m_prev, lrun, acc)
        if slot >= 1:
            h = slot - 1
            o = acc / jnp.sum(lrun, axis=-1, keepdims=True)
            if h % 2 == 0:
                pair_out = o
            else:
                hp = h // 2
                o_ref[0, :, hp * LANES:(hp + 1) * LANES] = jnp.where(lane < V_DIM, pair_out, o).astype(BF16)
        if slot < MLA_HEADS:
            m_prev = jnp.max(mrun, axis=-1, keepdims=True)


def _attention(q, ckr, lw, *, tq, tc):
    B, S, _ = q.shape
    T, W = ckr.shape[1:]
    return pl.pallas_call(
        functools.partial(_attention_kernel, tq=tq, T=T, tc=tc),
        out_shape=jax.ShapeDtypeStruct((B, S, MLA_HEADS * V_DIM), BF16),
        grid=(B, S // tq),
        in_specs=[pl.BlockSpec((1, tq, MLA_HEADS * HEAD_PAD), lambda b, i: (b, i, 0)),
                  pl.BlockSpec((1, T, W), lambda b, i: (b, 0, 0)),
                  _const_spec((MLA_HEADS * HEAD_PAD, W)),
                  _const_spec((KV_LORA, MLA_HEADS * V_DIM))],
        out_specs=pl.BlockSpec((1, tq, MLA_HEADS * V_DIM), lambda b, i: (b, i, 0)),
        scratch_shapes=[pltpu.VMEM((MLA_HEADS * HEAD_PAD, T), BF16), pltpu.VMEM((T, MLA_HEADS * V_DIM), BF16),
                        pltpu.VMEM((2, tq, T), F32)],
        compiler_params=_params(("arbitrary", "arbitrary")),
        name="attention",
    )(q, ckr, lw['wk_t'], lw['w_v'])


def _post_mixer_kernel(x_ref, attn_ref, pool_ref, sgu_ref, mod_ref, wout_ref, gpost_ref, gffn_ref, x1_ref, h2_ref):
    mix = jnp.concatenate([attn_ref[0], pool_ref[0], sgu_ref[0]], axis=-1)
    out = _dot(mix, wout_ref[...])
    x1 = x_ref[0] + mod_ref[0, 2:3, :] * _rms(out, gpost_ref[...])
    x1_ref[0] = x1
    h2 = _rms(x1, gffn_ref[...]) * (1 + mod_ref[0, 4:5, :]) + mod_ref[0, 3:4, :]
    h2_ref[0] = h2.astype(BF16)


def _post_mixer(x, attn, pool, sgu, mod, lw, *, latent, tm):
    B, S, D = x.shape
    mod_idx = (lambda b, i: (b + 1, 0, 0)) if latent else (lambda b, i: (0, 0, 0))
    tok = lambda b, i: (b, i, 0)
    return pl.pallas_call(
        _post_mixer_kernel,
        out_shape=[jax.ShapeDtypeStruct((B, S, D), F32), jax.ShapeDtypeStruct((B, S, D), BF16)],
        grid=(B, S // tm),
        in_specs=[pl.BlockSpec((1, tm, D), tok),
                  pl.BlockSpec((1, tm, attn.shape[-1]), tok),
                  pl.BlockSpec((1, tm, POOL_W), tok),
                  pl.BlockSpec((1, tm, SGU_W), tok),
                  pl.BlockSpec((1, 8, D), mod_idx),
                  _const_spec(lw['w_out'].shape),
                  _const_spec((1, D)),
                  _const_spec((1, D))],
        out_specs=[pl.BlockSpec((1, tm, D), tok), pl.BlockSpec((1, tm, D), tok)],
        compiler_params=_params(("parallel", "parallel")),
        name="post_mixer",
    )(x, attn, pool, sgu, mod, lw['w_out'], lw['g_post_mix'], lw['g_pre_ffn'])


def _interleave(a, b):
    out, ia, ib = [], 0, 0
    while ia < len(a) or ib < len(b):
        if ib >= len(b) or (ia < len(a) and ia * len(b) <= ib * len(a)):
            out.append(a[ia])
            ia += 1
        else:
            out.append(b[ib])
            ib += 1
    return out


def _conv_ffn_kernel(h_ref, hprev_ref, hnext_ref, x1_ref, mod_ref, wup_ref, cw_ref, wdown_ref, gpost_ref, o_ref,
                     hext_ref, zg0_ref, zv0_ref, zg1_ref, zv1_ref, a0_ref, a1_ref, acc_ref, *, tm, n_chunks, rb):
    i = pl.program_id(1)
    halo = BF16_SUBLANES
    D = hext_ref.shape[1]
    zero = jnp.zeros((), BF16)
    hext_ref[0:halo, :] = jnp.where(i > 0, hprev_ref[0], zero)
    hext_ref[halo:halo + tm, :] = h_ref[0]
    hext_ref[halo + tm:halo + tm + halo, :] = jnp.where(i < pl.num_programs(1) - 1, hnext_ref[0], zero)
    acc_ref[...] = jnp.zeros_like(acc_ref)
    zs = ((zg0_ref, zv0_ref), (zg1_ref, zv1_ref))
    as_ = (a0_ref, a1_ref)

    def conv_pair(z_ref, slab, cw, r0):
        n = rb // 2
        base = halo + r0
        e_m = z_ref[slab, pl.ds(base - 1, n, stride=2), :]
        e_c = z_ref[slab, pl.ds(base, n, stride=2), :]
        e_p = z_ref[slab, pl.ds(base + 1, n, stride=2), :]
        e_pp = z_ref[slab, pl.ds(base + 2, n, stride=2), :]
        w0, w1, w2, b = cw[0:1, :], cw[1:2, :], cw[2:3, :], cw[3:4, :]
        even = e_m * w0 + e_c * w1 + e_p * w2 + b
        odd = e_c * w0 + e_p * w1 + e_pp * w2 + b
        return even, odd

    def up_pieces(c, z_pair):
        pieces = []
        for which in range(2):
            state = {}

            def piece(k, which=which, state=state):
                part = _dot(hext_ref[:, k * MXU_DIM:(k + 1) * MXU_DIM],
                            wup_ref[which * n_chunks + c, k * MXU_DIM:(k + 1) * MXU_DIM, :])
                state['z'] = part if k == 0 else state['z'] + part
                if k == D // MXU_DIM - 1:
                    for slab in range(FF_CHUNK // LANES):
                        z_pair[which][slab] = state['z'][:, slab * LANES:(slab + 1) * LANES]

            pieces += [functools.partial(piece, k) for k in range(D // MXU_DIM)]
        return pieces

    def gate_pieces(c, z_pair, a_ref):
        def piece(r0):
            cwg = cw_ref[c]
            cwv = cw_ref[n_chunks + c]
            for slab in range(FF_CHUNK // LANES):
                ls = slice(slab * LANES, (slab + 1) * LANES)
                g_e, g_o = conv_pair(z_pair[0], slab, cwg[:, ls], r0)
                v_e, v_o = conv_pair(z_pair[1], slab, cwv[:, ls], r0)
                a_ref[r0 // 2:(r0 + rb) // 2, ls] = pltpu.pack_elementwise(
                    [jax.nn.silu(g_e) * v_e, jax.nn.silu(g_o) * v_o], packed_dtype=BF16)

        return [functools.partial(piece, r * rb) for r in range(tm // rb)]

    def down_pieces(c, a_ref):
        def piece(n):
            ns = slice(n * MXU_DIM, (n + 1) * MXU_DIM)
            acc_ref[:, ns] += _dot(pltpu.bitcast(a_ref[...], BF16), wdown_ref[c, :, ns])

        return [functools.partial(piece, n) for n in range(D // MXU_DIM)]

    for s in range(n_chunks + 2):
        par = s % 2
        mxu, vpu = [], []
        if s < n_chunks:
            mxu += up_pieces(s, zs[par])
        if 1 <= s <= n_chunks:
            vpu += gate_pieces(s - 1, zs[1 - par], as_[1 - par])
        if s >= 2:
            mxu += down_pieces(s - 2, as_[par])
        for emit in _interleave(mxu, vpu):
            emit()
    o_ref[0] = x1_ref[0] + mod_ref[0, 5:6, :] * _rms(acc_ref[...], gpost_ref[...])


def _conv_ffn(h2, x1, mod, lw, *, latent, tm):
    B, S, D = x1.shape
    n_chunks = lw['w_down'].shape[0]
    halo = BF16_SUBLANES
    nblk = tm // halo
    last = S // halo - 1
    mod_idx = (lambda b, i: (b + 1, 0, 0)) if latent else (lambda b, i: (0, 0, 0))
    tok = lambda b, i: (b, i, 0)
    z_scratch = pltpu.VMEM((FF_CHUNK // LANES, tm + 2 * halo, LANES), F32)
    a_scratch = pltpu.VMEM((tm // 2, FF_CHUNK), jnp.uint32)
    return pl.pallas_call(
        functools.partial(_conv_ffn_kernel, tm=tm, n_chunks=n_chunks, rb=FF_GATE_ROWS),
        out_shape=jax.ShapeDtypeStruct((B, S, D), F32),
        grid=(B, S // tm),
        in_specs=[pl.BlockSpec((1, tm, D), tok),
                  pl.BlockSpec((1, halo, D), lambda b, i: (b, jnp.maximum(i * nblk - 1, 0), 0)),
                  pl.BlockSpec((1, halo, D), lambda b, i: (b, jnp.minimum((i + 1) * nblk, last), 0)),
                  pl.BlockSpec((1, tm, D), tok),
                  pl.BlockSpec((1, 8, D), mod_idx),
                  _const_spec(lw['w_up'].shape),
                  _const_spec(lw['conv'].shape),
                  _const_spec(lw['w_down'].shape),
                  _const_spec((1, D))],
        out_specs=pl.BlockSpec((1, tm, D), tok),
        scratch_shapes=[pltpu.VMEM((tm + 2 * halo, D), BF16), z_scratch, z_scratch, z_scratch, z_scratch,
                        a_scratch, a_scratch, pltpu.VMEM((tm, D), F32)],
        compiler_params=_params(("parallel", "parallel")),
        name="conv_ffn",
    )(h2, h2, h2, x1, mod, lw['w_up'], lw['conv'], lw['w_down'], lw['g_post_ffn'])


def _rope_tables(S):
    n_rows = S // GRID_W
    rows = jnp.repeat(jnp.arange(n_rows, dtype=F32), GRID_W)
    cols = jnp.tile(jnp.arange(GRID_W, dtype=F32), n_rows)
    freqs = ROPE_BASE ** (-jnp.arange(ROPE_FREQS, dtype=F32) / ROPE_FREQS)
    ang = jnp.stack([rows[:, None] * freqs, cols[:, None] * freqs], axis=1)
    cos, sin = jnp.cos(ang), jnp.sin(ang)
    ck = jnp.stack([cos, cos], axis=2).reshape(S, ROPE_DIM)
    sk = jnp.stack([-sin, sin], axis=2).reshape(S, ROPE_DIM)
    pad = jnp.zeros((S, HEAD_PAD - NOPE_DIM - ROPE_DIM), F32)
    cq = jnp.concatenate([jnp.full((S, NOPE_DIM), Q_SCALE, F32), ck * Q_SCALE, pad], axis=1)
    sq = jnp.concatenate([jnp.zeros((S, NOPE_DIM), F32), sk * Q_SCALE, pad], axis=1)
    return cq, sq, ck, sk


def _swap_halves(w):
    shp = w.shape
    w4 = w.reshape(*shp[:-1], 2, 2, ROPE_FREQS)
    return w4[..., ::-1, :].reshape(shp)


def _layer_weights(l, g_pre_mix, g_post_mix, g_pre_ffn, g_post_ffn, w_in, g_q_a, w_q_b, g_kv_a, w_kv_b,
                   w_pool, pool_scale, g_sgu, w_sgu, b_sgu, w_out, w_up, conv_w, conv_b, w_down):
    D = w_in.shape[1]
    o1 = Q_LORA
    o2 = o1 + KV_LORA
    o3 = o2 + ROPE_DIM
    o4 = o3 + POOL_W
    wi = w_in[l]
    w_kr = wi[:, o2:o3]
    zpad = jnp.zeros((D, LANES - ROPE_DIM), F32)
    w_in_r = jnp.concatenate([wi[:, :o2], wi[:, o3:], w_kr, zpad, _swap_halves(w_kr), zpad], axis=1)

    qd = NOPE_DIM + ROPE_DIM
    wq = w_q_b[l].reshape(Q_LORA, MLA_HEADS, qd)
    hz = jnp.zeros((Q_LORA, MLA_HEADS, HEAD_PAD - qd), F32)
    wq_main = jnp.concatenate([wq, hz], axis=-1).reshape(Q_LORA, MLA_HEADS * HEAD_PAD)
    wq_sw = jnp.concatenate([jnp.zeros((Q_LORA, MLA_HEADS, NOPE_DIM), F32), _swap_halves(wq[..., NOPE_DIM:]), hz],
                            axis=-1).reshape(Q_LORA, MLA_HEADS * HEAD_PAD)

    wkv = w_kv_b[l].reshape(KV_LORA, MLA_HEADS, NOPE_DIM + V_DIM)
    wk_t = jnp.transpose(wkv[..., :NOPE_DIM], (1, 2, 0))
    wk_full = jnp.zeros((MLA_HEADS, HEAD_PAD, KV_LORA + ROPE_DIM), F32)
    wk_full = wk_full.at[:, :NOPE_DIM, :KV_LORA].set(wk_t)
    wk_full = wk_full.at[:, NOPE_DIM:qd, KV_LORA:].set(jnp.eye(ROPE_DIM, dtype=F32))
    w_v = wkv[..., NOPE_DIM:].reshape(KV_LORA, MLA_HEADS * V_DIM)

    wp = jnp.zeros((POOL_W, POOL_W), F32)
    for g in range(len(POOL_WINDOWS)):
        wp = wp.at[g * POOL_GC:(g + 1) * POOL_GC, g * POOL_GC:(g + 1) * POOL_GC].set(w_pool[l, g])

    d_ff = w_down.shape[1]
    nch = d_ff // FF_CHUNK
    wu = w_up[l].reshape(D, 2 * nch, FF_CHUNK).transpose(1, 0, 2)
    conv = jnp.concatenate([conv_w[l], conv_b[l][None, :], jnp.zeros((4, 2 * d_ff), F32)], axis=0)
    conv = conv.reshape(8, 2 * nch, FF_CHUNK).transpose(1, 0, 2)
    row = lambda a: a[l][None, :]
    return {
        'g_pre_mix': row(g_pre_mix), 'g_post_mix': row(g_post_mix),
        'g_pre_ffn': row(g_pre_ffn), 'g_post_ffn': row(g_post_ffn),
        'w_in': w_in_r.astype(BF16), 'g_q_a': row(g_q_a),
        'w_q_b': jnp.concatenate([wq_main, wq_sw], axis=1).astype(BF16),
        'g_kv_a': row(g_kv_a), 'wk_t': wk_full.reshape(MLA_HEADS * HEAD_PAD, -1).astype(BF16), 'w_v': w_v.astype(BF16),
        'w_pool': wp.astype(BF16), 'pool_scale': row(pool_scale),
        'g_sgu': row(g_sgu), 'w_sgu': w_sgu[l].astype(BF16),
        'b_sgu': jnp.repeat(b_sgu[l].T, SGU_HD, axis=1),
        'w_out': w_out[l].astype(BF16), 'w_up': wu.astype(BF16), 'conv': conv,
        'w_down': w_down[l].reshape(nch, FF_CHUNK, D).astype(BF16),
    }


def _layer(x, mod, lw, tabs, ctx_ckr, *, latent, tm, tq):
    q, ckr, pool_in, sgu, *cache = _pre_mixer(x, mod, lw, tabs, latent=latent, tm=tm)
    if latent:
        ckr = jnp.concatenate([ctx_ckr.astype(BF16), ckr], axis=1)
    attn = _attention(q, ckr, lw, tq=tq, tc=min(512, ckr.shape[1]))
    pool = _pool(pool_in, lw)
    x1, h2 = _post_mixer(x, attn, pool, sgu, mod, lw, latent=latent, tm=tm)
    return _conv_ffn(h2, x1, mod, lw, latent=latent, tm=tm), cache


def kernel(x_prompt, x_sample, cache_ckv, cache_krope, c, c_ctx, w_mod, b_mod, g_pre_mix, g_post_mix, g_pre_ffn, g_post_ffn, w_in, g_q_a, w_q_b, g_kv_a, w_kv_b, w_pool, pool_scale, g_sgu, w_sgu, b_sgu, w_out, w_up, conv_w, conv_b, w_down):
    depth = w_in.shape[0]
    D = x_prompt.shape[-1]
    n_lat = c.shape[0]
    mod_rows = 16
    cc = jnp.concatenate([c_ctx[None, :], c, jnp.zeros((mod_rows - 1 - n_lat, D), F32)], axis=0)
    mod_all = _modulation(cc, w_mod, b_mod)
    mod_all = jnp.pad(mod_all.reshape(depth, mod_rows, 6, D), ((0, 0), (0, 0), (0, 2), (0, 0)))
    tabs = _rope_tables(x_sample.shape[1])

    xp, xs = x_prompt, x_sample
    ckv_list, kr_list = [], []
    for l in range(depth):
        lw = _layer_weights(l, g_pre_mix, g_post_mix, g_pre_ffn, g_post_ffn, w_in, g_q_a, w_q_b, g_kv_a, w_kv_b,
                            w_pool, pool_scale, g_sgu, w_sgu, b_sgu, w_out, w_up, conv_w, conv_b, w_down)
        xp, (ckv, kr) = _layer(xp, mod_all[l], lw, None, None, latent=False, tm=256, tq=256)
        ckv_list.append(ckv)
        kr_list.append(kr)
        ctx_ckr = jnp.concatenate([cache_ckv[:, l], cache_krope[:, l]], axis=-1)
        xs, _ = _layer(xs, mod_all[l], lw, tabs, ctx_ckr, latent=True, tm=512, tq=256)
    return (xp, xs, jnp.stack(ckv_list, axis=1), jnp.stack(kr_list, axis=1))
```

```python
import functools

import jax
import jax.numpy as jnp
from jax import lax
from jax.experimental import pallas as pl
from jax.experimental.pallas import tpu as pltpu

F32 = jnp.float32
BF16 = jnp.bfloat16

GRID_W = 64
MLA_HEADS = 8
NOPE_DIM = 64
ROPE_DIM = 32
ROPE_FREQS = ROPE_DIM // 4
V_DIM = 64
Q_LORA = 256
KV_LORA = 128
ROPE_BASE = 10000.0
SM_SCALE = (NOPE_DIM + ROPE_DIM) ** -0.5
LOG2E = 1.4426950408889634
Q_SCALE = SM_SCALE * LOG2E
POOL_WINDOWS = (2, 4, 8, 16)
POOL_GC = 64
POOL_W = POOL_GC * len(POOL_WINDOWS)
SGU_HEADS = 4
SGU_HD = 64
SGU_W = SGU_HEADS * SGU_HD
CHUNK = 128
EPS = 1e-6

LANES = 128
BF16_SUBLANES = 16
HEAD_PAD = LANES
VMEM_LIMIT = 62 * 1024 * 1024

C_QA = 0
C_KVA = C_QA + Q_LORA
C_POOL = C_KVA + KV_LORA
C_U = C_POOL + POOL_W
C_V = C_U + SGU_W
C_KR = C_V + SGU_W
IN_COLS_PAD = C_KR + LANES

FF_CHUNK = 256
FF_GATE_ROWS = 64
MXU_DIM = 256
POOL_PAD = 16
POOL_ROWS = 256


def _rms(x, g):
    return x * lax.rsqrt(jnp.mean(x * x, axis=-1, keepdims=True) + EPS) * g


def _dot(a, b):
    return jnp.dot(a, b, preferred_element_type=F32)


def _params(sem):
    return pltpu.CompilerParams(dimension_semantics=sem, vmem_limit_bytes=VMEM_LIMIT)


def _const_spec(shape):
    nd = len(shape)
    return pl.BlockSpec(shape, lambda *_: (0,) * nd)


def _mod_kernel(c_ref, w_ref, b_ref, o_ref):
    a = jax.nn.silu(c_ref[...]).astype(BF16)
    o_ref[0] = _dot(a, w_ref[0].astype(BF16)) + b_ref[0]


def _modulation(cc, w_mod, b_mod):
    depth, d, e = w_mod.shape
    rows = cc.shape[0]
    te = 1536
    return pl.pallas_call(
        _mod_kernel,
        out_shape=jax.ShapeDtypeStruct((depth, rows, e), F32),
        grid=(depth, e // te),
        in_specs=[
            pl.BlockSpec((rows, d), lambda l, j: (0, 0)),
            pl.BlockSpec((1, d, te), lambda l, j: (l, 0, j)),
            pl.BlockSpec((1, 1, te), lambda l, j: (l, 0, j)),
        ],
        out_specs=pl.BlockSpec((1, rows, te), lambda l, j: (l, 0, j)),
        compiler_params=_params(("parallel", "parallel")),
        name="modulation",
    )(cc, w_mod, b_mod.reshape(depth, 1, e))


def _rope(x, tabs):
    c_ref, up_ref, dn_ref = tabs
    up = pltpu.roll(x, LANES - ROPE_FREQS, axis=1)
    dn = pltpu.roll(x, ROPE_FREQS, axis=1)
    return x * c_ref[...] + up * up_ref[...] + dn * dn_ref[...]


def _pre_mixer_kernel(*refs, latent, tm):
    if latent:
        (x_ref, mod_ref, gpre_ref, win_ref, gqa_ref, wqb_ref, gkva_ref, gsgu_ref, wsgu_ref, bsgu_ref,
         *tabs, q_ref, ckr_ref, pool_ref, sgu_ref) = refs
    else:
        (x_ref, mod_ref, gpre_ref, win_ref, gqa_ref, wqb_ref, gkva_ref, gsgu_ref, wsgu_ref, bsgu_ref,
         q_ref, ckr_ref, pool_ref, sgu_ref, ckv_ref, kr_ref) = refs
    x = x_ref[0]
    shift = mod_ref[0, 0:1, :]
    scale = mod_ref[0, 1:2, :]
    h = _rms(x, gpre_ref[...]) * (1 + scale) + shift
    proj = _dot(h.astype(BF16), win_ref[...])

    qn = _rms(proj[:, C_QA:C_QA + Q_LORA], gqa_ref[...]).astype(BF16)
    qq = _dot(qn, wqb_ref[...])
    for hh in range(MLA_HEADS):
        lo = hh * HEAD_PAD
        qh = qq[:, lo:lo + HEAD_PAD]
        qh = _rope(qh, tabs[:3]) if latent else qh * Q_SCALE
        q_ref[0, :, lo:lo + HEAD_PAD] = qh.astype(BF16)

    ckv = _rms(proj[:, C_KVA:C_KVA + KV_LORA], gkva_ref[...])
    kr = proj[:, C_KR:C_KR + LANES]
    if latent:
        kr = _rope(kr, tabs[3:])
    kr = kr[:, :ROPE_DIM]
    if not latent:
        ckv_ref[0] = ckv
        kr_ref[0] = kr
    ckr_ref[0, :, 0:KV_LORA] = ckv.astype(BF16)
    ckr_ref[0, :, KV_LORA:KV_LORA + ROPE_DIM] = kr.astype(BF16)
    pool_ref[0] = proj[:, C_POOL:C_POOL + POOL_W]

    u = proj[:, C_U:C_U + SGU_W]
    vn = _rms(proj[:, C_V:C_V + SGU_W], gsgu_ref[...]).astype(BF16)
    lane = lax.broadcasted_iota(jnp.int32, (1, LANES), 1)
    zero = jnp.zeros((), BF16)
    for c in range(tm // CHUNK):
        r0 = c * CHUNK
        for pair in range(SGU_W // LANES):
            l0 = pair * LANES
            vp = vn[r0:r0 + CHUNK, l0:l0 + LANES]
            v_lo = jnp.where(lane < SGU_HD, vp, zero)
            v_hi = jnp.where(lane >= SGU_HD, vp, zero)
            z = _dot(wsgu_ref[2 * pair], v_lo) + _dot(wsgu_ref[2 * pair + 1], v_hi) + bsgu_ref[:, l0:l0 + LANES]
            sgu_ref[0, r0:r0 + CHUNK, l0:l0 + LANES] = (u[r0:r0 + CHUNK, l0:l0 + LANES] * z).astype(BF16)


def _pre_mixer(x, mod, lw, tabs, *, latent, tm):
    B, S, D = x.shape
    mod_idx = (lambda i, b: (b + 1, 0, 0)) if latent else (lambda i, b: (0, 0, 0))
    tok = lambda i, b: (b, i, 0)
    in_specs = [
        pl.BlockSpec((1, tm, D), tok),
        pl.BlockSpec((1, 8, D), mod_idx),
        _const_spec((1, D)),
        _const_spec(lw['w_in'].shape),
        _const_spec((1, Q_LORA)),
        _const_spec(lw['w_q_b'].shape),
        _const_spec((1, KV_LORA)),
        _const_spec((1, SGU_W)),
        _const_spec((SGU_HEADS, CHUNK, CHUNK)),
        _const_spec((CHUNK, SGU_W)),
    ]
    args = [x, mod, lw['g_pre_mix'], lw['w_in'], lw['g_q_a'], lw['w_q_b'], lw['g_kv_a'], lw['g_sgu'],
            lw['w_sgu'], lw['b_sgu']]
    if latent:
        in_specs += [pl.BlockSpec((tm, LANES), lambda i, b: (i, 0)) for _ in tabs]
        args += list(tabs)
    out_shape = [
        jax.ShapeDtypeStruct((B, S, MLA_HEADS * HEAD_PAD), BF16),
        jax.ShapeDtypeStruct((B, S, KV_LORA + ROPE_DIM), BF16),
        jax.ShapeDtypeStruct((B, S, POOL_W), F32),
        jax.ShapeDtypeStruct((B, S, SGU_W), BF16),
    ]
    out_specs = [
        pl.BlockSpec((1, tm, MLA_HEADS * HEAD_PAD), tok),
        pl.BlockSpec((1, tm, KV_LORA + ROPE_DIM), tok),
        pl.BlockSpec((1, tm, POOL_W), tok),
        pl.BlockSpec((1, tm, SGU_W), tok),
    ]
    if not latent:
        out_shape += [jax.ShapeDtypeStruct((B, S, KV_LORA), F32), jax.ShapeDtypeStruct((B, S, ROPE_DIM), F32)]
        out_specs += [pl.BlockSpec((1, tm, KV_LORA), tok), pl.BlockSpec((1, tm, ROPE_DIM), tok)]
    return pl.pallas_call(
        functools.partial(_pre_mixer_kernel, latent=latent, tm=tm),
        out_shape=out_shape,
        grid=(S // tm, B),
        in_specs=in_specs,
        out_specs=out_specs,
        compiler_params=_params(("parallel", "parallel")),
        name="pre_mixer_lat" if latent else "pre_mixer_ctx",
    )(*args)


def _pool_kernel(x_ref, w_ref, scale_ref, o_ref, pad_ref, *, S):
    nslab = POOL_W // LANES
    zeros = jnp.zeros((POOL_PAD, LANES), F32)
    for slab in range(nslab):
        pad_ref[slab, 0:POOL_PAD, :] = zeros
        pad_ref[slab, POOL_PAD + S:POOL_PAD + S + POOL_PAD, :] = zeros
        pad_ref[slab, POOL_PAD:POOL_PAD + S, :] = x_ref[0, :, slab * LANES:(slab + 1) * LANES]

    R = min(POOL_ROWS, S)
    n = R // 2
    lane = lax.broadcasted_iota(jnp.int32, (1, LANES), 1)
    lane_w = lax.broadcasted_iota(jnp.int32, (1, POOL_W), 1)
    half = jnp.where(lane_w < POOL_GC, POOL_WINDOWS[0] // 2,
                     jnp.where(lane_w < 2 * POOL_GC, POOL_WINDOWS[1] // 2,
                               jnp.where(lane_w < 3 * POOL_GC, POOL_WINDOWS[2] // 2, POOL_WINDOWS[3] // 2)))
    for r in range(S // R):
        base = POOL_PAD + r * R

        def window_sums(slab, lo_small, lo_big, par):
            def ld(off):
                return pad_ref[slab, pl.ds(base + par + off, n, stride=2), :]

            x = ld(0)
            small = x
            for off in range(-lo_small, lo_small):
                if off != 0:
                    small = small + ld(off)
            big = small
            for off in list(range(-lo_big, -lo_small)) + list(range(lo_small, lo_big)):
                big = big + ld(off)
            return x, jnp.where(lane < POOL_GC, small, big)

        ds = []
        for par in range(2):
            xa, tot_a = window_sums(0, POOL_WINDOWS[0] // 2, POOL_WINDOWS[1] // 2, par)
            xb, tot_b = window_sums(1, POOL_WINDOWS[2] // 2, POOL_WINDOWS[3] // 2, par)
            t = r * R + par + 2 * lax.broadcasted_iota(jnp.int32, (n, 1), 0)
            cnt = (jnp.minimum(t + half, S) - jnp.maximum(t - half, 0)).astype(F32)
            ds.append(jnp.concatenate([tot_a, tot_b], axis=1) / cnt - jnp.concatenate([xa, xb], axis=1))
        d = pltpu.bitcast(pltpu.pack_elementwise(ds, packed_dtype=BF16), BF16)
        y = _dot(d, w_ref[...]) * scale_ref[...]
        o_ref[0, r * R:(r + 1) * R, :] = y.astype(BF16)


def _pool(pool_in, lw):
    B, S, W = pool_in.shape
    return pl.pallas_call(
        functools.partial(_pool_kernel, S=S),
        out_shape=jax.ShapeDtypeStruct((B, S, W), BF16),
        grid=(B,),
        in_specs=[pl.BlockSpec((1, S, W), lambda b: (b, 0, 0)), _const_spec((W, W)), _const_spec((1, W))],
        out_specs=pl.BlockSpec((1, S, W), lambda b: (b, 0, 0)),
        scratch_shapes=[pltpu.VMEM((W // LANES, S + 2 * POOL_PAD, LANES), F32)],
        compiler_params=_params(("parallel",)),
        name="pool",
    )(pool_in, lw['w_pool'], lw['pool_scale'])


def _attention_kernel(q_ref, ckr_ref, wk_ref, wv_ref, o_ref, kt_ref, v_ref, s_ref, *, tq, T, tc):
    nc = T // tc
    lane = lax.broadcasted_iota(jnp.int32, (1, LANES), 1)

    @pl.when(pl.program_id(1) == 0)
    def _():
        for c in range(nc):
            a = ckr_ref[0, c * tc:(c + 1) * tc, :]
            kt = lax.dot_general(wk_ref[...], a, (((1,), (1,)), ((), ())), preferred_element_type=F32)
            kt_ref[:, c * tc:(c + 1) * tc] = kt.astype(BF16)
            v_ref[c * tc:(c + 1) * tc, :] = _dot(a[:, :KV_LORA], wv_ref[...]).astype(BF16)

    def scores(h, c, mrun):
        s = _dot(q_ref[0, :, h * HEAD_PAD:(h + 1) * HEAD_PAD],
                 kt_ref[h * HEAD_PAD:(h + 1) * HEAD_PAD, c * tc:(c + 1) * tc])
        s_ref[h % 2, :, c * tc:(c + 1) * tc] = s
        for g in range(tc // LANES):
            mrun = jnp.maximum(mrun, s[:, g * LANES:(g + 1) * LANES])
        return mrun

    def probs_pv(h, c, m, lrun, acc):
        p = jnp.exp2(s_ref[h % 2, :, c * tc:(c + 1) * tc] - m)
        for g in range(tc // LANES):
            lrun = lrun + p[:, g * LANES:(g + 1) * LANES]
        hp = h // 2
        acc = acc + _dot(p.astype(BF16), v_ref[c * tc:(c + 1) * tc, hp * LANES:(hp + 1) * LANES])
        return lrun, acc

    m_prev = None
    pair_out = None
    for slot in range(MLA_HEADS + 1):
        mrun = jnp.full((tq, LANES), -jnp.inf, F32)
        lrun = jnp.zeros((tq, LANES), F32)
        acc = jnp.zeros((tq, LANES), F32)
        for c in range(nc):
            if slot < MLA_HEADS:
                mrun = scores(slot, c, mrun)
            if slot >= 1:
                lrun, acc = probs_pv(slot - 1, c, m_prev, lrun, acc)
        if slot >= 1:
            h = slot - 1
            o = acc / jnp.sum(lrun, axis=-1, keepdims=True)
            if h % 2 == 0:
                pair_out = o
            else:
                hp = h // 2
                o_ref[0, :, hp * LANES:(hp + 1) * LANES] = jnp.where(lane < V_DIM, pair_out, o).astype(BF16)
        if slot < MLA_HEADS:
            m_prev = jnp.max(mrun, axis=-1, keepdims=True)


def _attention(q, ckr, lw, *, tq, tc):
    B, S, _ = q.shape
    T, W = ckr.shape[1:]
    return pl.pallas_call(
        functools.partial(_attention_kernel, tq=tq, T=T, tc=tc),
        out_shape=jax.ShapeDtypeStruct((B, S, MLA_HEADS * V_DIM), BF16),
        grid=(B, S // tq),
        in_specs=[pl.BlockSpec((1, tq, MLA_HEADS * HEAD_PAD), lambda b, i: (b, i, 0)),
                  pl.BlockSpec((1, T, W), lambda b, i: (b, 0, 0)),
                  _const_spec((MLA_HEADS * HEAD_PAD, W)),
                  _const_spec((KV_LORA, MLA_HEADS * V_DIM))],
        out_specs=pl.BlockSpec((1, tq, MLA_HEADS * V_DIM), lambda b, i: (b, i, 0)),
        scratch_shapes=[pltpu.VMEM((MLA_HEADS * HEAD_PAD, T), BF16), pltpu.VMEM((T, MLA_HEADS * V_DIM), BF16),
                        pltpu.VMEM((2, tq, T), F32)],
        compiler_params=_params(("arbitrary", "arbitrary")),
        name="attention",
    )(q, ckr, lw['wk_t'], lw['w_v'])


def _post_mixer_kernel(x_ref, attn_ref, pool_ref, sgu_ref, mod_ref, wout_ref, gpost_ref, gffn_ref, x1_ref, h2_ref):
    mix = jnp.concatenate([attn_ref[0], pool_ref[0], sgu_ref[0]], axis=-1)
    out = _dot(mix, wout_ref[...])
    x1 = x_ref[0] + mod_ref[0, 2:3, :] * _rms(out, gpost_ref[...])
    x1_ref[0] = x1
    h2 = _rms(x1, gffn_ref[...]) * (1 + mod_ref[0, 4:5, :]) + mod_ref[0, 3:4, :]
    h2_ref[0] = h2.astype(BF16)


def _post_mixer(x, attn, pool, sgu, mod, lw, *, latent, tm):
    B, S, D = x.shape
    mod_idx = (lambda b, i: (b + 1, 0, 0)) if latent else (lambda b, i: (0, 0, 0))
    tok = lambda b, i: (b, i, 0)
    return pl.pallas_call(
        _post_mixer_kernel,
        out_shape=[jax.ShapeDtypeStruct((B, S, D), F32), jax.ShapeDtypeStruct((B, S, D), BF16)],
        grid=(B, S // tm),
        in_specs=[pl.BlockSpec((1, tm, D), tok),
                  pl.BlockSpec((1, tm, attn.shape[-1]), tok),
                  pl.BlockSpec((1, tm, POOL_W), tok),
                  pl.BlockSpec((1, tm, SGU_W), tok),
                  pl.BlockSpec((1, 8, D), mod_idx),
                  _const_spec(lw['w_out'].shape),
                  _const_spec((1, D)),
                  _const_spec((1, D))],
        out_specs=[pl.BlockSpec((1, tm, D), tok), pl.BlockSpec((1, tm, D), tok)],
        compiler_params=_params(("parallel", "parallel")),
        name="post_mixer",
    )(x, attn, pool, sgu, mod, lw['w_out'], lw['g_post_mix'], lw['g_pre_ffn'])


def _interleave(a, b):
    out, ia, ib = [], 0, 0
    while ia < len(a) or ib < len(b):
        if ib >= len(b) or (ia < len(a) and ia * len(b) <= ib * len(a)):
            out.append(a[ia])
            ia += 1
        else:
            out.append(b[ib])
            ib += 1
    return out


def _conv_ffn_kernel(h_ref, hprev_ref, hnext_ref, x1_ref, mod_ref, wup_ref, cw_ref, wdown_ref, gpost_ref, o_ref,
                     hext_ref, zg0_ref, zv0_ref, zg1_ref, zv1_ref, a0_ref, a1_ref, acc_ref, *, tm, n_chunks, rb):
    i = pl.program_id(1)
    halo = BF16_SUBLANES
    D = hext_ref.shape[1]
    zero = jnp.zeros((), BF16)
    hext_ref[0:halo, :] = jnp.where(i > 0, hprev_ref[0], zero)
    hext_ref[halo:halo + tm, :] = h_ref[0]
    hext_ref[halo + tm:halo + tm + halo, :] = jnp.where(i < pl.num_programs(1) - 1, hnext_ref[0], zero)
    acc_ref[...] = jnp.zeros_like(acc_ref)
    zs = ((zg0_ref, zv0_ref), (zg1_ref, zv1_ref))
    as_ = (a0_ref, a1_ref)

    def conv_pair(z_ref, slab, cw, r0):
        n = rb // 2
        base = halo + r0
        e_m = z_ref[slab, pl.ds(base - 1, n, stride=2), :]
        e_c = z_ref[slab, pl.ds(base, n, stride=2), :]
        e_p = z_ref[slab, pl.ds(base + 1, n, stride=2), :]
        e_pp = z_ref[slab, pl.ds(base + 2, n, stride=2), :]
        w0, w1, w2, b = cw[0:1, :], cw[1:2, :], cw[2:3, :], cw[3:4, :]
        even = e_m * w0 + e_c * w1 + e_p * w2 + b
        odd = e_c * w0 + e_p * w1 + e_pp * w2 + b
        return even, odd

    def up_pieces(c, z_pair):
        pieces = []
        for which in range(2):
            state = {}

            def piece(k, which=which, state=state):
                part = _dot(hext_ref[:, k * MXU_DIM:(k + 1) * MXU_DIM],
                            wup_ref[which * n_chunks + c, k * MXU_DIM:(k + 1) * MXU_DIM, :])
                state['z'] = part if k == 0 else state['z'] + part
                if k == D // MXU_DIM - 1:
                    for slab in range(FF_CHUNK // LANES):
                        z_pair[which][slab] = state['z'][:, slab * LANES:(slab + 1) * LANES]

            pieces += [functools.partial(piece, k) for k in range(D // MXU_DIM)]
        return pieces

    def gate_pieces(c, z_pair, a_ref):
        def piece(r0):
            cwg = cw_ref[c]
            cwv = cw_ref[n_chunks + c]
            for slab in range(FF_CHUNK // LANES):
                ls = slice(slab * LANES, (slab + 1) * LANES)
                g_e, g_o = conv_pair(z_pair[0], slab, cwg[:, ls], r0)
                v_e, v_o = conv_pair(z_pair[1], slab, cwv[:, ls], r0)
                a_ref[r0 // 2:(r0 + rb) // 2, ls] = pltpu.pack_elementwise(
                    [jax.nn.silu(g_e) * v_e, jax.nn.silu(g_o) * v_o], packed_dtype=BF16)

        return [functools.partial(piece, r * rb) for r in range(tm // rb)]

    def down_pieces(c, a_ref):
        def piece(n):
            ns = slice(n * MXU_DIM, (n + 1) * MXU_DIM)
            acc_ref[:, ns] += _dot(pltpu.bitcast(a_ref[...], BF16), wdown_ref[c, :, ns])

        return [functools.partial(piece, n) for n in range(D // MXU_DIM)]

    for s in range(n_chunks + 2):
        par = s % 2
        mxu, vpu = [], []
        if s < n_chunks:
            mxu += up_pieces(s, zs[par])
        if 1 <= s <= n_chunks:
            vpu += gate_pieces(s - 1, zs[1 - par], as_[1 - par])
        if s >= 2:
            mxu += down_pieces(s - 2, as_[par])
        for emit in _interleave(mxu, vpu):
            emit()
    o_ref[0] = x1_ref[0] + mod_ref[0, 5:6, :] * _rms(acc_ref[...], gpost_ref[...])


def _conv_ffn(h2, x1, mod, lw, *, latent, tm):
    B, S, D = x1.shape
    n_chunks = lw['w_down'].shape[0]
    halo = BF16_SUBLANES
    nblk = tm // halo
    last = S // halo - 1
    mod_idx = (lambda b, i: (b + 1, 0, 0)) if latent else (lambda b, i: (0, 0, 0))
    tok = lambda b, i: (b, i, 0)
    z_scratch = pltpu.VMEM((FF_CHUNK // LANES, tm + 2 * halo, LANES), F32)
    a_scratch = pltpu.VMEM((tm // 2, FF_CHUNK), jnp.uint32)
    return pl.pallas_call(
        functools.partial(_conv_ffn_kernel, tm=tm, n_chunks=n_chunks, rb=FF_GATE_ROWS),
        out_shape=jax.ShapeDtypeStruct((B, S, D), F32),
        grid=(B, S // tm),
        in_specs=[pl.BlockSpec((1, tm, D), tok),
                  pl.BlockSpec((1, halo, D), lambda b, i: (b, jnp.maximum(i * nblk - 1, 0), 0)),
                  pl.BlockSpec((1, halo, D), lambda b, i: (b, jnp.minimum((i + 1) * nblk, last), 0)),
                  pl.BlockSpec((1, tm, D), tok),
                  pl.BlockSpec((1, 8, D), mod_idx),
                  _const_spec(lw['w_up'].shape),
                  _const_spec(lw['conv'].shape),
                  _const_spec(lw['w_down'].shape),
                  _const_spec((1, D))],
        out_specs=pl.BlockSpec((1, tm, D), tok),
        scratch_shapes=[pltpu.VMEM((tm + 2 * halo, D), BF16), z_scratch, z_scratch, z_scratch, z_scratch,
                        a_scratch, a_scratch, pltpu.VMEM((tm, D), F32)],
        compiler_params=_params(("parallel", "parallel")),
        name="conv_ffn",
    )(h2, h2, h2, x1, mod, lw['w_up'], lw['conv'], lw['w_down'], lw['g_post_ffn'])


def _rope_tables(S):
    n_rows = S // GRID_W
    rows = jnp.repeat(jnp.arange(n_rows, dtype=F32), GRID_W)
    cols = jnp.tile(jnp.arange(GRID_W, dtype=F32), n_rows)
    freqs = ROPE_BASE ** (-jnp.arange(ROPE_FREQS, dtype=F32) / ROPE_FREQS)
    ang = jnp.stack([rows[:, None] * freqs, cols[:, None] * freqs], axis=1)
    cos, sin = jnp.cos(ang), jnp.sin(ang)
    zero = jnp.zeros_like(sin)
    c = jnp.stack([cos, cos], axis=2).reshape(S, ROPE_DIM)
    s_up = jnp.stack([-sin, zero], axis=2).reshape(S, ROPE_DIM)
    s_dn = jnp.stack([zero, sin], axis=2).reshape(S, ROPE_DIM)

    def widen(t, lead, lead_value, scale):
        pads = [jnp.full((S, lead), lead_value, F32), t * scale, jnp.zeros((S, LANES - lead - ROPE_DIM), F32)]
        return jnp.concatenate(pads, axis=1)

    q_tabs = (widen(c, NOPE_DIM, Q_SCALE, Q_SCALE), widen(s_up, NOPE_DIM, 0.0, Q_SCALE),
              widen(s_dn, NOPE_DIM, 0.0, Q_SCALE))
    k_tabs = (widen(c, 0, 0.0, 1.0), widen(s_up, 0, 0.0, 1.0), widen(s_dn, 0, 0.0, 1.0))
    return q_tabs + k_tabs


def _layer_weights(l, g_pre_mix, g_post_mix, g_pre_ffn, g_post_ffn, w_in, g_q_a, w_q_b, g_kv_a, w_kv_b,
                   w_pool, pool_scale, g_sgu, w_sgu, b_sgu, w_out, w_up, conv_w, conv_b, w_down):
    D = w_in.shape[1]
    o1 = Q_LORA
    o2 = o1 + KV_LORA
    o3 = o2 + ROPE_DIM
    o4 = o3 + POOL_W
    wi = w_in[l]
    w_in_r = jnp.concatenate([wi[:, :o2], wi[:, o3:], wi[:, o2:o3], jnp.zeros((D, LANES - ROPE_DIM), F32)], axis=1)

    qd = NOPE_DIM + ROPE_DIM
    wq = w_q_b[l].reshape(Q_LORA, MLA_HEADS, qd)
    hz = jnp.zeros((Q_LORA, MLA_HEADS, HEAD_PAD - qd), F32)
    wq_pad = jnp.concatenate([wq, hz], axis=-1).reshape(Q_LORA, MLA_HEADS * HEAD_PAD)

    wkv = w_kv_b[l].reshape(KV_LORA, MLA_HEADS, NOPE_DIM + V_DIM)
    wk_t = jnp.transpose(wkv[..., :NOPE_DIM], (1, 2, 0))
    wk_full = jnp.zeros((MLA_HEADS, HEAD_PAD, KV_LORA + ROPE_DIM), F32)
    wk_full = wk_full.at[:, :NOPE_DIM, :KV_LORA].set(wk_t)
    wk_full = wk_full.at[:, NOPE_DIM:qd, KV_LORA:].set(jnp.eye(ROPE_DIM, dtype=F32))
    w_v = wkv[..., NOPE_DIM:].reshape(KV_LORA, MLA_HEADS * V_DIM)

    wp = jnp.zeros((POOL_W, POOL_W), F32)
    for g in range(len(POOL_WINDOWS)):
        wp = wp.at[g * POOL_GC:(g + 1) * POOL_GC, g * POOL_GC:(g + 1) * POOL_GC].set(w_pool[l, g])

    d_ff = w_down.shape[1]
    nch = d_ff // FF_CHUNK
    wu = w_up[l].reshape(D, 2 * nch, FF_CHUNK).transpose(1, 0, 2)
    conv = jnp.concatenate([conv_w[l], conv_b[l][None, :], jnp.zeros((4, 2 * d_ff), F32)], axis=0)
    conv = conv.reshape(8, 2 * nch, FF_CHUNK).transpose(1, 0, 2)
    row = lambda a: a[l][None, :]
    return {
        'g_pre_mix': row(g_pre_mix), 'g_post_mix': row(g_post_mix),
        'g_pre_ffn': row(g_pre_ffn), 'g_post_ffn': row(g_post_ffn),
        'w_in': w_in_r.astype(BF16), 'g_q_a': row(g_q_a),
        'w_q_b': wq_pad.astype(BF16),
        'g_kv_a': row(g_kv_a), 'wk_t': wk_full.reshape(MLA_HEADS * HEAD_PAD, -1).astype(BF16), 'w_v': w_v.astype(BF16),
        'w_pool': wp.astype(BF16), 'pool_scale': row(pool_scale),
        'g_sgu': row(g_sgu), 'w_sgu': w_sgu[l].astype(BF16),
        'b_sgu': jnp.repeat(b_sgu[l].T, SGU_HD, axis=1),
        'w_out': w_out[l].astype(BF16), 'w_up': wu.astype(BF16), 'conv': conv,
        'w_down': w_down[l].reshape(nch, FF_CHUNK, D).astype(BF16),
    }


def _layer(x, mod, lw, tabs, ctx_ckr, *, latent, tm, tq):
    q, ckr, pool_in, sgu, *cache = _pre_mixer(x, mod, lw, tabs, latent=latent, tm=tm)
    if latent:
        ckr = jnp.concatenate([ctx_ckr.astype(BF16), ckr], axis=1)
    attn = _attention(q, ckr, lw, tq=tq, tc=min(512, ckr.shape[1]))
    pool = _pool(pool_in, lw)
    x1, h2 = _post_mixer(x, attn, pool, sgu, mod, lw, latent=latent, tm=tm)
    return _conv_ffn(h2, x1, mod, lw, latent=latent, tm=tm), cache


def kernel(x_prompt, x_sample, cache_ckv, cache_krope, c, c_ctx, w_mod, b_mod, g_pre_mix, g_post_mix, g_pre_ffn, g_post_ffn, w_in, g_q_a, w_q_b, g_kv_a, w_kv_b, w_pool, pool_scale, g_sgu, w_sgu, b_sgu, w_out, w_up, conv_w, conv_b, w_down):
    depth = w_in.shape[0]
    D = x_prompt.shape[-1]
    n_lat = c.shape[0]
    mod_rows = 16
    cc = jnp.concatenate([c_ctx[None, :], c, jnp.zeros((mod_rows - 1 - n_lat, D), F32)], axis=0)
    mod_all = _modulation(cc, w_mod, b_mod)
    mod_all = jnp.pad(mod_all.reshape(depth, mod_rows, 6, D), ((0, 0), (0, 0), (0, 2), (0, 0)))
    tabs = _rope_tables(x_sample.shape[1])

    xp, xs = x_prompt, x_sample
    ckv_list, kr_list = [], []
    for l in range(depth):
        lw = _layer_weights(l, g_pre_mix, g_post_mix, g_pre_ffn, g_post_ffn, w_in, g_q_a, w_q_b, g_kv_a, w_kv_b,
                            w_pool, pool_scale, g_sgu, w_sgu, b_sgu, w_out, w_up, conv_w, conv_b, w_down)
        xp, (ckv, kr) = _layer(xp, mod_all[l], lw, None, None, latent=False, tm=256, tq=256)
        ckv_list.append(ckv)
        kr_list.append(kr)
        ctx_ckr = jnp.concatenate([cache_ckv[:, l], cache_krope[:, l]], axis=-1)
        xs, _ = _layer(xs, mod_all[l], lw, tabs, ctx_ckr, latent=True, tm=512, tq=512)
    return (xp, xs, jnp.stack(ckv_list, axis=1), jnp.stack(kr_list, axis=1))
```

```python
import functools

import jax
import jax.numpy as jnp
from jax import lax
from jax.experimental import pallas as pl
from jax.experimental.pallas import tpu as pltpu

F32 = jnp.float32
BF16 = jnp.bfloat16

GRID_W = 64
MLA_HEADS = 8
NOPE_DIM = 64
ROPE_DIM = 32
ROPE_FREQS = ROPE_DIM // 4
V_DIM = 64
Q_LORA = 256
KV_LORA = 128
ROPE_BASE = 10000.0
SM_SCALE = (NOPE_DIM + ROPE_DIM) ** -0.5
LOG2E = 1.4426950408889634
Q_SCALE = SM_SCALE * LOG2E
POOL_WINDOWS = (2, 4, 8, 16)
POOL_GC = 64
POOL_W = POOL_GC * len(POOL_WINDOWS)
SGU_HEADS = 4
SGU_HD = 64
SGU_W = SGU_HEADS * SGU_HD
CHUNK = 128
EPS = 1e-6

LANES = 128
BF16_SUBLANES = 16
HEAD_PAD = LANES
VMEM_LIMIT = 62 * 1024 * 1024

C_QA = 0
C_KVA = C_QA + Q_LORA
C_POOL = C_KVA + KV_LORA
C_U = C_POOL + POOL_W
C_V = C_U + SGU_W
C_KR = C_V + SGU_W
IN_COLS_PAD = C_KR + LANES

FF_CHUNK = 256
FF_GATE_ROWS = 64
MXU_DIM = 256
POOL_PAD = 16
POOL_ROWS = 256


def _rms(x, g):
    return x * lax.rsqrt(jnp.mean(x * x, axis=-1, keepdims=True) + EPS) * g


def _dot(a, b):
    return jnp.dot(a, b, preferred_element_type=F32)


def _params(sem):
    return pltpu.CompilerParams(dimension_semantics=sem, vmem_limit_bytes=VMEM_LIMIT)


def _const_spec(shape):
    nd = len(shape)
    return pl.BlockSpec(shape, lambda *_: (0,) * nd)


def _mod_kernel(c_ref, w_ref, b_ref, o_ref):
    a = jax.nn.silu(c_ref[...]).astype(BF16)
    o_ref[0] = _dot(a, w_ref[0].astype(BF16)) + b_ref[0]


def _modulation(cc, w_mod, b_mod):
    depth, d, e = w_mod.shape
    rows = cc.shape[0]
    te = 1536
    return pl.pallas_call(
        _mod_kernel,
        out_shape=jax.ShapeDtypeStruct((depth, rows, e), F32),
        grid=(depth, e // te),
        in_specs=[
            pl.BlockSpec((rows, d), lambda l, j: (0, 0)),
            pl.BlockSpec((1, d, te), lambda l, j: (l, 0, j)),
            pl.BlockSpec((1, 1, te), lambda l, j: (l, 0, j)),
        ],
        out_specs=pl.BlockSpec((1, rows, te), lambda l, j: (l, 0, j)),
        compiler_params=_params(("parallel", "parallel")),
        name="modulation",
    )(cc, w_mod, b_mod.reshape(depth, 1, e))


def _rope(x, tabs):
    c_ref, up_ref, dn_ref = tabs
    up = pltpu.roll(x, LANES - ROPE_FREQS, axis=1)
    dn = pltpu.roll(x, ROPE_FREQS, axis=1)
    return x * c_ref[...] + up * up_ref[...] + dn * dn_ref[...]


def _pre_mixer_kernel(*refs, latent, tm):
    if latent:
        (x_ref, mod_ref, gpre_ref, win_ref, gqa_ref, wqb_ref, gkva_ref, gsgu_ref, wsgu_ref, bsgu_ref,
         *tabs, q_ref, ckr_ref, pool_ref, sgu_ref) = refs
    else:
        (x_ref, mod_ref, gpre_ref, win_ref, gqa_ref, wqb_ref, gkva_ref, gsgu_ref, wsgu_ref, bsgu_ref,
         q_ref, ckr_ref, pool_ref, sgu_ref, ckv_ref, kr_ref) = refs
    x = x_ref[0]
    shift = mod_ref[0, 0:1, :]
    scale = mod_ref[0, 1:2, :]
    h = _rms(x, gpre_ref[...]) * (1 + scale) + shift
    proj = _dot(h.astype(BF16), win_ref[...])

    qn = _rms(proj[:, C_QA:C_QA + Q_LORA], gqa_ref[...]).astype(BF16)
    if latent:
        qq = _dot(qn, wqb_ref[...])
    else:
        qq = _dot(qn, wqb_ref[:, 0:MLA_HEADS * HEAD_PAD])
    width = MLA_HEADS * HEAD_PAD
    for hh in range(MLA_HEADS):
        lo = hh * HEAD_PAD
        if latent:
            qh = qq[:, lo:lo + HEAD_PAD] * tabs[0][...] + qq[:, width + lo:width + lo + HEAD_PAD] * tabs[1][...]
        else:
            qh = qq[:, lo:lo + HEAD_PAD] * Q_SCALE
        q_ref[0, :, lo:lo + HEAD_PAD] = qh.astype(BF16)

    ckv = _rms(proj[:, C_KVA:C_KVA + KV_LORA], gkva_ref[...])
    kr = proj[:, C_KR:C_KR + LANES]
    if latent:
        kr = _rope(kr, tabs[2:])
    kr = kr[:, :ROPE_DIM]
    if not latent:
        ckv_ref[0] = ckv
        kr_ref[0] = kr
    ckr_ref[0, :, 0:KV_LORA] = ckv.astype(BF16)
    ckr_ref[0, :, KV_LORA:KV_LORA + ROPE_DIM] = kr.astype(BF16)
    pool_ref[0] = proj[:, C_POOL:C_POOL + POOL_W]

    u = proj[:, C_U:C_U + SGU_W]
    vn = _rms(proj[:, C_V:C_V + SGU_W], gsgu_ref[...]).astype(BF16)
    lane = lax.broadcasted_iota(jnp.int32, (1, LANES), 1)
    zero = jnp.zeros((), BF16)
    for c in range(tm // CHUNK):
        r0 = c * CHUNK
        for pair in range(SGU_W // LANES):
            l0 = pair * LANES
            vp = vn[r0:r0 + CHUNK, l0:l0 + LANES]
            v_lo = jnp.where(lane < SGU_HD, vp, zero)
            v_hi = jnp.where(lane >= SGU_HD, vp, zero)
            z = _dot(wsgu_ref[2 * pair], v_lo) + _dot(wsgu_ref[2 * pair + 1], v_hi) + bsgu_ref[:, l0:l0 + LANES]
            sgu_ref[0, r0:r0 + CHUNK, l0:l0 + LANES] = (u[r0:r0 + CHUNK, l0:l0 + LANES] * z).astype(BF16)


def _pre_mixer(x, mod, lw, tabs, *, latent, tm):
    B, S, D = x.shape
    mod_idx = (lambda i, b: (b + 1, 0, 0)) if latent else (lambda i, b: (0, 0, 0))
    tok = lambda i, b: (b, i, 0)
    in_specs = [
        pl.BlockSpec((1, tm, D), tok),
        pl.BlockSpec((1, 8, D), mod_idx),
        _const_spec((1, D)),
        _const_spec(lw['w_in'].shape),
        _const_spec((1, Q_LORA)),
        _const_spec(lw['w_q_b'].shape),
        _const_spec((1, KV_LORA)),
        _const_spec((1, SGU_W)),
        _const_spec((SGU_HEADS, CHUNK, CHUNK)),
        _const_spec((CHUNK, SGU_W)),
    ]
    args = [x, mod, lw['g_pre_mix'], lw['w_in'], lw['g_q_a'], lw['w_q_b'], lw['g_kv_a'], lw['g_sgu'],
            lw['w_sgu'], lw['b_sgu']]
    if latent:
        in_specs += [pl.BlockSpec((tm, LANES), lambda i, b: (i, 0)) for _ in tabs]
        args += list(tabs)
    out_shape = [
        jax.ShapeDtypeStruct((B, S, MLA_HEADS * HEAD_PAD), BF16),
        jax.ShapeDtypeStruct((B, S, KV_LORA + ROPE_DIM), BF16),
        jax.ShapeDtypeStruct((B, S, POOL_W), F32),
        jax.ShapeDtypeStruct((B, S, SGU_W), BF16),
    ]
    out_specs = [
        pl.BlockSpec((1, tm, MLA_HEADS * HEAD_PAD), tok),
        pl.BlockSpec((1, tm, KV_LORA + ROPE_DIM), tok),
        pl.BlockSpec((1, tm, POOL_W), tok),
        pl.BlockSpec((1, tm, SGU_W), tok),
    ]
    if not latent:
        out_shape += [jax.ShapeDtypeStruct((B, S, KV_LORA), F32), jax.ShapeDtypeStruct((B, S, ROPE_DIM), F32)]
        out_specs += [pl.BlockSpec((1, tm, KV_LORA), tok), pl.BlockSpec((1, tm, ROPE_DIM), tok)]
    return pl.pallas_call(
        functools.partial(_pre_mixer_kernel, latent=latent, tm=tm),
        out_shape=out_shape,
        grid=(S // tm, B),
        in_specs=in_specs,
        out_specs=out_specs,
        compiler_params=_params(("parallel", "parallel")),
        name="pre_mixer_lat" if latent else "pre_mixer_ctx",
    )(*args)


def _pool_kernel(x_ref, w_ref, scale_ref, o_ref, pad_ref, *, S):
    nslab = POOL_W // LANES
    zeros = jnp.zeros((POOL_PAD, LANES), F32)
    for slab in range(nslab):
        pad_ref[slab, 0:POOL_PAD, :] = zeros
        pad_ref[slab, POOL_PAD + S:POOL_PAD + S + POOL_PAD, :] = zeros
        pad_ref[slab, POOL_PAD:POOL_PAD + S, :] = x_ref[0, :, slab * LANES:(slab + 1) * LANES]

    R = min(POOL_ROWS, S)
    n = R // 2
    lane = lax.broadcasted_iota(jnp.int32, (1, LANES), 1)
    lane_w = lax.broadcasted_iota(jnp.int32, (1, POOL_W), 1)
    half = jnp.where(lane_w < POOL_GC, POOL_WINDOWS[0] // 2,
                     jnp.where(lane_w < 2 * POOL_GC, POOL_WINDOWS[1] // 2,
                               jnp.where(lane_w < 3 * POOL_GC, POOL_WINDOWS[2] // 2, POOL_WINDOWS[3] // 2)))
    for r in range(S // R):
        base = POOL_PAD + r * R

        def window_sums(slab, lo_small, lo_big, par):
            def ld(off):
                return pad_ref[slab, pl.ds(base + par + off, n, stride=2), :]

            x = ld(0)
            small = x
            for off in range(-lo_small, lo_small):
                if off != 0:
                    small = small + ld(off)
            big = small
            for off in list(range(-lo_big, -lo_small)) + list(range(lo_small, lo_big)):
                big = big + ld(off)
            return x, jnp.where(lane < POOL_GC, small, big)

        ds = []
        for par in range(2):
            xa, tot_a = window_sums(0, POOL_WINDOWS[0] // 2, POOL_WINDOWS[1] // 2, par)
            xb, tot_b = window_sums(1, POOL_WINDOWS[2] // 2, POOL_WINDOWS[3] // 2, par)
            t = r * R + par + 2 * lax.broadcasted_iota(jnp.int32, (n, 1), 0)
            cnt = (jnp.minimum(t + half, S) - jnp.maximum(t - half, 0)).astype(F32)
            ds.append(jnp.concatenate([tot_a, tot_b], axis=1) / cnt - jnp.concatenate([xa, xb], axis=1))
        d = pltpu.bitcast(pltpu.pack_elementwise(ds, packed_dtype=BF16), BF16)
        y = _dot(d, w_ref[...]) * scale_ref[...]
        o_ref[0, r * R:(r + 1) * R, :] = y.astype(BF16)


def _pool(pool_in, lw):
    B, S, W = pool_in.shape
    return pl.pallas_call(
        functools.partial(_pool_kernel, S=S),
        out_shape=jax.ShapeDtypeStruct((B, S, W), BF16),
        grid=(B,),
        in_specs=[pl.BlockSpec((1, S, W), lambda b: (b, 0, 0)), _const_spec((W, W)), _const_spec((1, W))],
        out_specs=pl.BlockSpec((1, S, W), lambda b: (b, 0, 0)),
        scratch_shapes=[pltpu.VMEM((W // LANES, S + 2 * POOL_PAD, LANES), F32)],
        compiler_params=_params(("parallel",)),
        name="pool",
    )(pool_in, lw['w_pool'], lw['pool_scale'])


def _attention_kernel(q_ref, ckr_ref, wk_ref, wv_ref, o_ref, kt_ref, v_ref, s_ref, *, tq, T, tc):
    nc = T // tc
    lane = lax.broadcasted_iota(jnp.int32, (1, LANES), 1)

    @pl.when(pl.program_id(1) == 0)
    def _():
        for c in range(nc):
            a = ckr_ref[0, c * tc:(c + 1) * tc, :]
            kt = lax.dot_general(wk_ref[...], a, (((1,), (1,)), ((), ())), preferred_element_type=F32)
            kt_ref[:, c * tc:(c + 1) * tc] = kt.astype(BF16)
            v_ref[c * tc:(c + 1) * tc, :] = _dot(a[:, :KV_LORA], wv_ref[...]).astype(BF16)

    def scores(h, c, mrun):
        s = _dot(q_ref[0, :, h * HEAD_PAD:(h + 1) * HEAD_PAD],
                 kt_ref[h * HEAD_PAD:(h + 1) * HEAD_PAD, c * tc:(c + 1) * tc])
        s_ref[h % 2, :, c * tc:(c + 1) * tc] = s
        for g in range(tc // LANES):
            mrun = jnp.maximum(mrun, s[:, g * LANES:(g + 1) * LANES])
        return mrun

    def probs_pv(h, c, m, lrun, acc):
        p = jnp.exp2(s_ref[h % 2, :, c * tc:(c + 1) * tc] - m)
        for g in range(tc // LANES):
            lrun = lrun + p[:, g * LANES:(g + 1) * LANES]
        hp = h // 2
        acc = acc + _dot(p.astype(BF16), v_ref[c * tc:(c + 1) * tc, hp * LANES:(hp + 1) * LANES])
        return lrun, acc

    m_prev = None
    pair_out = None
    for slot in range(MLA_HEADS + 1):
        mrun = jnp.full((tq, LANES), -jnp.inf, F32)
        lrun = jnp.zeros((tq, LANES), F32)
        acc = jnp.zeros((tq, LANES), F32)
        for c in range(nc):
            if slot < MLA_HEADS:
                mrun = scores(slot, c, mrun)
            if slot >= 1:
                lrun, acc = probs_pv(slot - 1, c, m_prev, lrun, acc)
        if slot >= 1:
            h = slot - 1
            o = acc / jnp.sum(lrun, axis=-1, keepdims=True)
            if h % 2 == 0:
                pair_out = o
            else:
                hp = h // 2
                o_ref[0, :, hp * LANES:(hp + 1) * LANES] = jnp.where(lane < V_DIM, pair_out, o).astype(BF16)
        if slot < MLA_HEADS:
            m_prev = jnp.max(mrun, axis=-1, keepdims=True)


def _attention(q, ckr, lw, *, tq, tc):
    B, S, _ = q.shape
    T, W = ckr.shape[1:]
    return pl.pallas_call(
        functools.partial(_attention_kernel, tq=tq, T=T, tc=tc),
        out_shape=jax.ShapeDtypeStruct((B, S, MLA_HEADS * V_DIM), BF16),
        grid=(B, S // tq),
        in_specs=[pl.BlockSpec((1, tq, MLA_HEADS * HEAD_PAD), lambda b, i: (b, i, 0)),
                  pl.BlockSpec((1, T, W), lambda b, i: (b, 0, 0)),
                  _const_spec((MLA_HEADS * HEAD_PAD, W)),
                  _const_spec((KV_LORA, MLA_HEADS * V_DIM))],
        out_specs=pl.BlockSpec((1, tq, MLA_HEADS * V_DIM), lambda b, i: (b, i, 0)),
        scratch_shapes=[pltpu.VMEM((MLA_HEADS * HEAD_PAD, T), BF16), pltpu.VMEM((T, MLA_HEADS * V_DIM), BF16),
                        pltpu.VMEM((2, tq, T), F32)],
        compiler_params=_params(("arbitrary", "arbitrary")),
        name="attention",
    )(q, ckr, lw['wk_t'], lw['w_v'])


def _post_mixer_kernel(x_ref, attn_ref, pool_ref, sgu_ref, mod_ref, wout_ref, gpost_ref, gffn_ref, x1_ref, h2_ref,
                       *, tm):
    D = x_ref.shape[-1]
    half = tm // 2
    rb = half // (D // MXU_DIM)

    def mix(r0):
        rows = slice(r0, r0 + half)
        return jnp.concatenate([attn_ref[0, rows, :], pool_ref[0, rows, :], sgu_ref[0, rows, :]], axis=-1)

    def finish(out, r0):
        x1 = x_ref[0, r0:r0 + out.shape[0], :] + mod_ref[0, 2:3, :] * _rms(out, gpost_ref[...])
        x1_ref[0, r0:r0 + out.shape[0], :] = x1
        h2 = _rms(x1, gffn_ref[...]) * (1 + mod_ref[0, 4:5, :]) + mod_ref[0, 3:4, :]
        h2_ref[0, r0:r0 + out.shape[0], :] = h2.astype(BF16)

    out_a = _dot(mix(0), wout_ref[...])
    mix_b = mix(half)
    cols = []
    for j in range(D // MXU_DIM):
        cols.append(_dot(mix_b, wout_ref[:, j * MXU_DIM:(j + 1) * MXU_DIM]))
        finish(out_a[j * rb:(j + 1) * rb, :], j * rb)
    finish(jnp.concatenate(cols, axis=-1), half)


def _post_mixer(x, attn, pool, sgu, mod, lw, *, latent, tm):
    B, S, D = x.shape
    mod_idx = (lambda b, i: (b + 1, 0, 0)) if latent else (lambda b, i: (0, 0, 0))
    tok = lambda b, i: (b, i, 0)
    return pl.pallas_call(
        functools.partial(_post_mixer_kernel, tm=tm),
        out_shape=[jax.ShapeDtypeStruct((B, S, D), F32), jax.ShapeDtypeStruct((B, S, D), BF16)],
        grid=(B, S // tm),
        in_specs=[pl.BlockSpec((1, tm, D), tok),
                  pl.BlockSpec((1, tm, attn.shape[-1]), tok),
                  pl.BlockSpec((1, tm, POOL_W), tok),
                  pl.BlockSpec((1, tm, SGU_W), tok),
                  pl.BlockSpec((1, 8, D), mod_idx),
                  _const_spec(lw['w_out'].shape),
                  _const_spec((1, D)),
                  _const_spec((1, D))],
        out_specs=[pl.BlockSpec((1, tm, D), tok), pl.BlockSpec((1, tm, D), tok)],
        compiler_params=_params(("parallel", "parallel")),
        name="post_mixer",
    )(x, attn, pool, sgu, mod, lw['w_out'], lw['g_post_mix'], lw['g_pre_ffn'])


def _interleave(a, b):
    out, ia, ib = [], 0, 0
    while ia < len(a) or ib < len(b):
        if ib >= len(b) or (ia < len(a) and ia * len(b) <= ib * len(a)):
            out.append(a[ia])
            ia += 1
        else:
            out.append(b[ib])
            ib += 1
    return out


def _conv_ffn_kernel(h_ref, hprev_ref, hnext_ref, x1_ref, mod_ref, wup_ref, cw_ref, wdown_ref, gpost_ref, o_ref,
                     hext_ref, zg0_ref, zv0_ref, zg1_ref, zv1_ref, a0_ref, a1_ref, acc_ref, *, tm, n_chunks, rb):
    i = pl.program_id(1)
    halo = BF16_SUBLANES
    D = hext_ref.shape[1]
    zero = jnp.zeros((), BF16)
    hext_ref[0:halo, :] = jnp.where(i > 0, hprev_ref[0], zero)
    hext_ref[halo:halo + tm, :] = h_ref[0]
    hext_ref[halo + tm:halo + tm + halo, :] = jnp.where(i < pl.num_programs(1) - 1, hnext_ref[0], zero)
    acc_ref[...] = jnp.zeros_like(acc_ref)
    zs = ((zg0_ref, zv0_ref), (zg1_ref, zv1_ref))
    as_ = (a0_ref, a1_ref)

    def conv_pair(z_ref, slab, cw, r0):
        n = rb // 2
        base = halo + r0
        e_m = z_ref[slab, pl.ds(base - 1, n, stride=2), :]
        e_c = z_ref[slab, pl.ds(base, n, stride=2), :]
        e_p = z_ref[slab, pl.ds(base + 1, n, stride=2), :]
        e_pp = z_ref[slab, pl.ds(base + 2, n, stride=2), :]
        w0, w1, w2, b = cw[0:1, :], cw[1:2, :], cw[2:3, :], cw[3:4, :]
        even = e_m * w0 + e_c * w1 + e_p * w2 + b
        odd = e_c * w0 + e_p * w1 + e_pp * w2 + b
        return even, odd

    def up_pieces(c, z_pair):
        pieces = []
        for which in range(2):
            state = {}

            def piece(k, which=which, state=state):
                col = (which * n_chunks + c) * FF_CHUNK
                part = _dot(hext_ref[:, k * MXU_DIM:(k + 1) * MXU_DIM],
                            wup_ref[k * MXU_DIM:(k + 1) * MXU_DIM, col:col + FF_CHUNK])
                state['z'] = part if k == 0 else state['z'] + part
                if k == D // MXU_DIM - 1:
                    for slab in range(FF_CHUNK // LANES):
                        z_pair[which][slab] = state['z'][:, slab * LANES:(slab + 1) * LANES]

            pieces += [functools.partial(piece, k) for k in range(D // MXU_DIM)]
        return pieces

    def gate_pieces(c, z_pair, a_ref):
        def piece(r0):
            for slab in range(FF_CHUNK // LANES):
                ls = slice(slab * LANES, (slab + 1) * LANES)
                col = c * FF_CHUNK + slab * LANES
                cwg = cw_ref[:, col:col + LANES]
                cwv = cw_ref[:, n_chunks * FF_CHUNK + col:n_chunks * FF_CHUNK + col + LANES]
                g_e, g_o = conv_pair(z_pair[0], slab, cwg, r0)
                v_e, v_o = conv_pair(z_pair[1], slab, cwv, r0)
                a_ref[r0 // 2:(r0 + rb) // 2, ls] = pltpu.pack_elementwise(
                    [jax.nn.silu(g_e) * v_e, jax.nn.silu(g_o) * v_o], packed_dtype=BF16)

        return [functools.partial(piece, r * rb) for r in range(tm // rb)]

    def down_pieces(c, a_ref):
        def piece(n):
            ns = slice(n * MXU_DIM, (n + 1) * MXU_DIM)
            acc_ref[:, ns] += _dot(pltpu.bitcast(a_ref[...], BF16), wdown_ref[c * FF_CHUNK:(c + 1) * FF_CHUNK, ns])

        return [functools.partial(piece, n) for n in range(D // MXU_DIM)]

    for s in range(n_chunks + 2):
        par = s % 2
        mxu, vpu = [], []
        if s < n_chunks:
            mxu += up_pieces(s, zs[par])
        if 1 <= s <= n_chunks:
            vpu += gate_pieces(s - 1, zs[1 - par], as_[1 - par])
        if s >= 2:
            mxu += down_pieces(s - 2, as_[par])
        for emit in _interleave(mxu, vpu):
            emit()
    o_ref[0] = x1_ref[0] + mod_ref[0, 5:6, :] * _rms(acc_ref[...], gpost_ref[...])


def _conv_ffn(h2, x1, mod, lw, *, latent, tm):
    B, S, D = x1.shape
    n_chunks = lw['w_down'].shape[0] // FF_CHUNK
    halo = BF16_SUBLANES
    nblk = tm // halo
    last = S // halo - 1
    mod_idx = (lambda b, i: (b + 1, 0, 0)) if latent else (lambda b, i: (0, 0, 0))
    tok = lambda b, i: (b, i, 0)
    z_scratch = pltpu.VMEM((FF_CHUNK // LANES, tm + 2 * halo, LANES), F32)
    a_scratch = pltpu.VMEM((tm // 2, FF_CHUNK), jnp.uint32)
    return pl.pallas_call(
        functools.partial(_conv_ffn_kernel, tm=tm, n_chunks=n_chunks, rb=FF_GATE_ROWS),
        out_shape=jax.ShapeDtypeStruct((B, S, D), F32),
        grid=(B, S // tm),
        in_specs=[pl.BlockSpec((1, tm, D), tok),
                  pl.BlockSpec((1, halo, D), lambda b, i: (b, jnp.maximum(i * nblk - 1, 0), 0)),
                  pl.BlockSpec((1, halo, D), lambda b, i: (b, jnp.minimum((i + 1) * nblk, last), 0)),
                  pl.BlockSpec((1, tm, D), tok),
                  pl.BlockSpec((1, 8, D), mod_idx),
                  _const_spec(lw['w_up'].shape),
                  _const_spec(lw['conv'].shape),
                  _const_spec(lw['w_down'].shape),
                  _const_spec((1, D))],
        out_specs=pl.BlockSpec((1, tm, D), tok),
        scratch_shapes=[pltpu.VMEM((tm + 2 * halo, D), BF16), z_scratch, z_scratch, z_scratch, z_scratch,
                        a_scratch, a_scratch, pltpu.VMEM((tm, D), F32)],
        compiler_params=_params(("parallel", "parallel")),
        name="conv_ffn",
    )(h2, h2, h2, x1, mod, lw['w_up'], lw['conv'], lw['w_down'], lw['g_post_ffn'])


def _rope_tables(S):
    n_rows = S // GRID_W
    rows = jnp.repeat(jnp.arange(n_rows, dtype=F32), GRID_W)
    cols = jnp.tile(jnp.arange(GRID_W, dtype=F32), n_rows)
    freqs = ROPE_BASE ** (-jnp.arange(ROPE_FREQS, dtype=F32) / ROPE_FREQS)
    ang = jnp.stack([rows[:, None] * freqs, cols[:, None] * freqs], axis=1)
    cos, sin = jnp.cos(ang), jnp.sin(ang)
    zero = jnp.zeros_like(sin)
    c = jnp.stack([cos, cos], axis=2).reshape(S, ROPE_DIM)
    s_up = jnp.stack([-sin, zero], axis=2).reshape(S, ROPE_DIM)
    s_dn = jnp.stack([zero, sin], axis=2).reshape(S, ROPE_DIM)

    def widen(t, lead, lead_value, scale):
        pads = [jnp.full((S, lead), lead_value, F32), t * scale, jnp.zeros((S, LANES - lead - ROPE_DIM), F32)]
        return jnp.concatenate(pads, axis=1)

    q_tabs = (widen(c, NOPE_DIM, Q_SCALE, Q_SCALE), widen(s_up + s_dn, NOPE_DIM, 0.0, Q_SCALE))
    k_tabs = (widen(c, 0, 0.0, 1.0), widen(s_up, 0, 0.0, 1.0), widen(s_dn, 0, 0.0, 1.0))
    return q_tabs + k_tabs


def _layer_weights(l, g_pre_mix, g_post_mix, g_pre_ffn, g_post_ffn, w_in, g_q_a, w_q_b, g_kv_a, w_kv_b,
                   w_pool, pool_scale, g_sgu, w_sgu, b_sgu, w_out, w_up, conv_w, conv_b, w_down):
    D = w_in.shape[1]
    o1 = Q_LORA
    o2 = o1 + KV_LORA
    o3 = o2 + ROPE_DIM
    o4 = o3 + POOL_W
    wi = w_in[l]
    w_in_r = jnp.concatenate([wi[:, :o2], wi[:, o3:], wi[:, o2:o3], jnp.zeros((D, LANES - ROPE_DIM), F32)], axis=1)

    qd = NOPE_DIM + ROPE_DIM
    wq = w_q_b[l].reshape(Q_LORA, MLA_HEADS, qd)
    hz = jnp.zeros((Q_LORA, MLA_HEADS, HEAD_PAD - qd), F32)
    wq_pad = jnp.concatenate([wq, hz], axis=-1).reshape(Q_LORA, MLA_HEADS * HEAD_PAD)
    wq_rope = wq[..., NOPE_DIM:].reshape(Q_LORA, MLA_HEADS, 2, 2, ROPE_FREQS)
    wq_partner = wq_rope[..., ::-1, :].reshape(Q_LORA, MLA_HEADS, ROPE_DIM)
    wq_sw = jnp.concatenate([jnp.zeros((Q_LORA, MLA_HEADS, NOPE_DIM), F32), wq_partner, hz],
                            axis=-1).reshape(Q_LORA, MLA_HEADS * HEAD_PAD)

    wkv = w_kv_b[l].reshape(KV_LORA, MLA_HEADS, NOPE_DIM + V_DIM)
    wk_t = jnp.transpose(wkv[..., :NOPE_DIM], (1, 2, 0))
    wk_full = jnp.zeros((MLA_HEADS, HEAD_PAD, KV_LORA + ROPE_DIM), F32)
    wk_full = wk_full.at[:, :NOPE_DIM, :KV_LORA].set(wk_t)
    wk_full = wk_full.at[:, NOPE_DIM:qd, KV_LORA:].set(jnp.eye(ROPE_DIM, dtype=F32))
    w_v = wkv[..., NOPE_DIM:].reshape(KV_LORA, MLA_HEADS * V_DIM)

    wp = jnp.zeros((POOL_W, POOL_W), F32)
    for g in range(len(POOL_WINDOWS)):
        wp = wp.at[g * POOL_GC:(g + 1) * POOL_GC, g * POOL_GC:(g + 1) * POOL_GC].set(w_pool[l, g])

    d_ff = w_down.shape[1]
    conv = jnp.concatenate([conv_w[l], conv_b[l][None, :], jnp.zeros((4, 2 * d_ff), F32)], axis=0)
    row = lambda a: a[l][None, :]
    return {
        'g_pre_mix': row(g_pre_mix), 'g_post_mix': row(g_post_mix),
        'g_pre_ffn': row(g_pre_ffn), 'g_post_ffn': row(g_post_ffn),
        'w_in': w_in_r.astype(BF16), 'g_q_a': row(g_q_a),
        'w_q_b': jnp.concatenate([wq_pad, wq_sw], axis=1).astype(BF16),
        'g_kv_a': row(g_kv_a), 'wk_t': wk_full.reshape(MLA_HEADS * HEAD_PAD, -1).astype(BF16), 'w_v': w_v.astype(BF16),
        'w_pool': wp.astype(BF16), 'pool_scale': row(pool_scale),
        'g_sgu': row(g_sgu), 'w_sgu': w_sgu[l].astype(BF16),
        'b_sgu': jnp.repeat(b_sgu[l].T, SGU_HD, axis=1),
        'w_out': w_out[l].astype(BF16), 'w_up': w_up[l].astype(BF16), 'conv': conv,
        'w_down': w_down[l].astype(BF16),
    }


def _layer(x, mod, lw, tabs, ctx_ckr, *, latent, tm, tq):
    q, ckr, pool_in, sgu, *cache = _pre_mixer(x, mod, lw, tabs, latent=latent, tm=tm)
    if latent:
        ckr = jnp.concatenate([ctx_ckr.astype(BF16), ckr], axis=1)
    attn = _attention(q, ckr, lw, tq=tq, tc=min(512, ckr.shape[1]))
    pool = _pool(pool_in, lw)
    x1, h2 = _post_mixer(x, attn, pool, sgu, mod, lw, latent=latent, tm=tm)
    return _conv_ffn(h2, x1, mod, lw, latent=latent, tm=tm), cache


def kernel(x_prompt, x_sample, cache_ckv, cache_krope, c, c_ctx, w_mod, b_mod, g_pre_mix, g_post_mix, g_pre_ffn, g_post_ffn, w_in, g_q_a, w_q_b, g_kv_a, w_kv_b, w_pool, pool_scale, g_sgu, w_sgu, b_sgu, w_out, w_up, conv_w, conv_b, w_down):
    depth = w_in.shape[0]
    D = x_prompt.shape[-1]
    n_lat = c.shape[0]
    mod_rows = 16
    cc = jnp.concatenate([c_ctx[None, :], c, jnp.zeros((mod_rows - 1 - n_lat, D), F32)], axis=0)
    mod_all = _modulation(cc, w_mod, b_mod)
    mod_all = jnp.pad(mod_all.reshape(depth, mod_rows, 6, D), ((0, 0), (0, 0), (0, 2), (0, 0)))
    tabs = _rope_tables(x_sample.shape[1])

    xp, xs = x_prompt, x_sample
    ckv_list, kr_list = [], []
    for l in range(depth):
        lw = _layer_weights(l, g_pre_mix, g_post_mix, g_pre_ffn, g_post_ffn, w_in, g_q_a, w_q_b, g_kv_a, w_kv_b,
                            w_pool, pool_scale, g_sgu, w_sgu, b_sgu, w_out, w_up, conv_w, conv_b, w_down)
        xp, (ckv, kr) = _layer(xp, mod_all[l], lw, None, None, latent=False, tm=256, tq=256)
        ckv_list.append(ckv)
        kr_list.append(kr)
        ctx_ckr = jnp.concatenate([cache_ckv[:, l], cache_krope[:, l]], axis=-1)
        xs, _ = _layer(xs, mod_all[l], lw, tabs, ctx_ckr, latent=True, tm=512, tq=256)
    return (xp, xs, jnp.stack(ckv_list, axis=1), jnp.stack(kr_list, axis=1))
```

```python
import functools

import jax
import jax.numpy as jnp
from jax import lax
from jax.experimental import pallas as pl
from jax.experimental.pallas import tpu as pltpu

F32 = jnp.float32
BF16 = jnp.bfloat16

GRID_W = 64
MLA_HEADS = 8
NOPE_DIM = 64
ROPE_DIM = 32
ROPE_FREQS = ROPE_DIM // 4
V_DIM = 64
Q_LORA = 256
KV_LORA = 128
ROPE_BASE = 10000.0
SM_SCALE = (NOPE_DIM + ROPE_DIM) ** -0.5
LOG2E = 1.4426950408889634
Q_SCALE = SM_SCALE * LOG2E
POOL_WINDOWS = (2, 4, 8, 16)
POOL_GC = 64
POOL_W = POOL_GC * len(POOL_WINDOWS)
SGU_HEADS = 4
SGU_HD = 64
SGU_W = SGU_HEADS * SGU_HD
CHUNK = 128
EPS = 1e-6

LANES = 128
BF16_SUBLANES = 16
HEAD_PAD = LANES
VMEM_LIMIT = 62 * 1024 * 1024

C_QA = 0
C_KVA = C_QA + Q_LORA
C_POOL = C_KVA + KV_LORA
C_U = C_POOL + POOL_W
C_V = C_U + SGU_W
C_KR = C_V + SGU_W
IN_COLS_PAD = C_KR + LANES

FF_CHUNK = 256
FF_GATE_ROWS = 64
MXU_DIM = 256
ATTN_KEYS = 512
ATTN_QUERIES = 256
TOKEN_TILE = 512
POOL_PAD = 16
POOL_ROWS = 256


def _rms(x, g):
    return x * lax.rsqrt(jnp.mean(x * x, axis=-1, keepdims=True) + EPS) * g


def _dot(a, b):
    return jnp.dot(a, b, preferred_element_type=F32)


def _params(sem):
    return pltpu.CompilerParams(dimension_semantics=sem, vmem_limit_bytes=VMEM_LIMIT)


def _const_spec(shape):
    nd = len(shape)
    return pl.BlockSpec(shape, lambda *_: (0,) * nd)


def _layer_spec(w, l):
    nd = w.ndim - 1
    return pl.BlockSpec((None,) + w.shape[1:], lambda *_: (l,) + (0,) * nd)


def _mod_kernel(c_ref, w_ref, b_ref, o_ref):
    a = jax.nn.silu(c_ref[...]).astype(BF16)
    o_ref[0] = _dot(a, w_ref[0].astype(BF16)) + b_ref[0]


def _modulation(cc, w_mod, b_mod):
    depth, d, e = w_mod.shape
    rows = cc.shape[0]
    te = 1536
    return pl.pallas_call(
        _mod_kernel,
        out_shape=jax.ShapeDtypeStruct((depth, rows, e), F32),
        grid=(depth, e // te),
        in_specs=[
            pl.BlockSpec((rows, d), lambda l, j: (0, 0)),
            pl.BlockSpec((1, d, te), lambda l, j: (l, 0, j)),
            pl.BlockSpec((1, 1, te), lambda l, j: (l, 0, j)),
        ],
        out_specs=pl.BlockSpec((1, rows, te), lambda l, j: (l, 0, j)),
        compiler_params=_params(("parallel", "parallel")),
        name="modulation",
    )(cc, w_mod, b_mod.reshape(depth, 1, e))


def _rope(x, tabs):
    c_ref, up_ref, dn_ref = tabs
    up = pltpu.roll(x, LANES - ROPE_FREQS, axis=1)
    dn = pltpu.roll(x, ROPE_FREQS, axis=1)
    return x * c_ref[...] + up * up_ref[...] + dn * dn_ref[...]


def _pre_mixer_kernel(*refs, latent, tm):
    if latent:
        (x_ref, mod_ref, gpre_ref, win_ref, gqa_ref, wqb_ref, gkva_ref, gsgu_ref, wsgu_ref, bsgu_ref,
         *tabs, q_ref, ckr_ref, pool_ref, sgu_ref) = refs
    else:
        (x_ref, mod_ref, gpre_ref, win_ref, gqa_ref, wqb_ref, gkva_ref, gsgu_ref, wsgu_ref, bsgu_ref,
         q_ref, ckr_ref, pool_ref, sgu_ref, ckv_ref, kr_ref) = refs
    x = x_ref[0]
    shift = mod_ref[0, 0:1, :]
    scale = mod_ref[0, 1:2, :]
    h = _rms(x, gpre_ref[...]) * (1 + scale) + shift
    proj = _dot(h.astype(BF16), win_ref[...])

    qn = _rms(proj[:, C_QA:C_QA + Q_LORA], gqa_ref[...]).astype(BF16)
    if latent:
        qq = _dot(qn, wqb_ref[...])
    else:
        qq = _dot(qn, wqb_ref[:, 0:MLA_HEADS * HEAD_PAD])
    width = MLA_HEADS * HEAD_PAD
    for hh in range(MLA_HEADS):
        lo = hh * HEAD_PAD
        if latent:
            qh = qq[:, lo:lo + HEAD_PAD] * tabs[0][...] + qq[:, width + lo:width + lo + HEAD_PAD] * tabs[1][...]
        else:
            qh = qq[:, lo:lo + HEAD_PAD] * Q_SCALE
        q_ref[0, :, lo:lo + HEAD_PAD] = qh.astype(BF16)

    ckv = _rms(proj[:, C_KVA:C_KVA + KV_LORA], gkva_ref[...])
    kr = proj[:, C_KR:C_KR + LANES]
    if latent:
        kr = _rope(kr, tabs[2:])
    kr = kr[:, :ROPE_DIM]
    if not latent:
        ckv_ref[0] = ckv
        kr_ref[0] = kr
    ckr_ref[0, :, 0:KV_LORA] = ckv.astype(BF16)
    ckr_ref[0, :, KV_LORA:KV_LORA + ROPE_DIM] = kr.astype(BF16)
    pool_ref[0] = proj[:, C_POOL:C_POOL + POOL_W]

    u = proj[:, C_U:C_U + SGU_W]
    vn = _rms(proj[:, C_V:C_V + SGU_W], gsgu_ref[...]).astype(BF16)
    lane = lax.broadcasted_iota(jnp.int32, (1, LANES), 1)
    zero = jnp.zeros((), BF16)
    for c in range(tm // CHUNK):
        r0 = c * CHUNK
        for pair in range(SGU_W // LANES):
            l0 = pair * LANES
            vp = vn[r0:r0 + CHUNK, l0:l0 + LANES]
            v_lo = jnp.where(lane < SGU_HD, vp, zero)
            v_hi = jnp.where(lane >= SGU_HD, vp, zero)
            z = _dot(wsgu_ref[2 * pair], v_lo) + _dot(wsgu_ref[2 * pair + 1], v_hi) + bsgu_ref[:, l0:l0 + LANES]
            sgu_ref[0, r0:r0 + CHUNK, l0:l0 + LANES] = (u[r0:r0 + CHUNK, l0:l0 + LANES] * z).astype(BF16)


def _pre_mixer(x, mod, lw, tabs, *, latent, tm):
    B, S, D = x.shape
    mod_idx = (lambda i, b: (b + 1, 0, 0)) if latent else (lambda i, b: (0, 0, 0))
    tok = lambda i, b: (b, i, 0)
    in_specs = [
        pl.BlockSpec((1, tm, D), tok),
        pl.BlockSpec((1, 8, D), mod_idx),
        _const_spec((1, D)),
        _const_spec(lw['w_in'].shape),
        _const_spec((1, Q_LORA)),
        _const_spec(lw['w_q_b'].shape),
        _const_spec((1, KV_LORA)),
        _const_spec((1, SGU_W)),
        _const_spec((SGU_HEADS, CHUNK, CHUNK)),
        _const_spec((CHUNK, SGU_W)),
    ]
    args = [x, mod, lw['g_pre_mix'], lw['w_in'], lw['g_q_a'], lw['w_q_b'], lw['g_kv_a'], lw['g_sgu'],
            lw['w_sgu'], lw['b_sgu']]
    if latent:
        in_specs += [pl.BlockSpec((tm, LANES), lambda i, b: (i, 0)) for _ in tabs]
        args += list(tabs)
    out_shape = [
        jax.ShapeDtypeStruct((B, S, MLA_HEADS * HEAD_PAD), BF16),
        jax.ShapeDtypeStruct((B, S, KV_LORA + ROPE_DIM), BF16),
        jax.ShapeDtypeStruct((B, S, POOL_W), F32),
        jax.ShapeDtypeStruct((B, S, SGU_W), BF16),
    ]
    out_specs = [
        pl.BlockSpec((1, tm, MLA_HEADS * HEAD_PAD), tok),
        pl.BlockSpec((1, tm, KV_LORA + ROPE_DIM), tok),
        pl.BlockSpec((1, tm, POOL_W), tok),
        pl.BlockSpec((1, tm, SGU_W), tok),
    ]
    if not latent:
        out_shape += [jax.ShapeDtypeStruct((B, S, KV_LORA), F32), jax.ShapeDtypeStruct((B, S, ROPE_DIM), F32)]
        out_specs += [pl.BlockSpec((1, tm, KV_LORA), tok), pl.BlockSpec((1, tm, ROPE_DIM), tok)]
    return pl.pallas_call(
        functools.partial(_pre_mixer_kernel, latent=latent, tm=tm),
        out_shape=out_shape,
        grid=(S // tm, B),
        in_specs=in_specs,
        out_specs=out_specs,
        compiler_params=_params(("parallel", "parallel")),
        name="pre_mixer_lat" if latent else "pre_mixer_ctx",
    )(*args)


def _pool_kernel(x_ref, w_ref, scale_ref, o_ref, pad_ref, *, S):
    nslab = POOL_W // LANES
    zeros = jnp.zeros((POOL_PAD, LANES), F32)
    for slab in range(nslab):
        pad_ref[slab, 0:POOL_PAD, :] = zeros
        pad_ref[slab, POOL_PAD + S:POOL_PAD + S + POOL_PAD, :] = zeros
        pad_ref[slab, POOL_PAD:POOL_PAD + S, :] = x_ref[0, :, slab * LANES:(slab + 1) * LANES]

    R = min(POOL_ROWS, S)
    n = R // 2
    lane = lax.broadcasted_iota(jnp.int32, (1, LANES), 1)
    lane_w = lax.broadcasted_iota(jnp.int32, (1, POOL_W), 1)
    half = jnp.where(lane_w < POOL_GC, POOL_WINDOWS[0] // 2,
                     jnp.where(lane_w < 2 * POOL_GC, POOL_WINDOWS[1] // 2,
                               jnp.where(lane_w < 3 * POOL_GC, POOL_WINDOWS[2] // 2, POOL_WINDOWS[3] // 2)))
    for r in range(S // R):
        base = POOL_PAD + r * R

        def window_sums(slab, lo_small, lo_big, par):
            def ld(off):
                return pad_ref[slab, pl.ds(base + par + off, n, stride=2), :]

            x = ld(0)
            small = x
            for off in range(-lo_small, lo_small):
                if off != 0:
                    small = small + ld(off)
            big = small
            for off in list(range(-lo_big, -lo_small)) + list(range(lo_small, lo_big)):
                big = big + ld(off)
            return x, jnp.where(lane < POOL_GC, small, big)

        ds = []
        for par in range(2):
            xa, tot_a = window_sums(0, POOL_WINDOWS[0] // 2, POOL_WINDOWS[1] // 2, par)
            xb, tot_b = window_sums(1, POOL_WINDOWS[2] // 2, POOL_WINDOWS[3] // 2, par)
            t = r * R + par + 2 * lax.broadcasted_iota(jnp.int32, (n, 1), 0)
            cnt = (jnp.minimum(t + half, S) - jnp.maximum(t - half, 0)).astype(F32)
            ds.append(jnp.concatenate([tot_a, tot_b], axis=1) / cnt - jnp.concatenate([xa, xb], axis=1))
        d = pltpu.bitcast(pltpu.pack_elementwise(ds, packed_dtype=BF16), BF16)
        y = _dot(d, w_ref[...]) * scale_ref[...]
        o_ref[0, r * R:(r + 1) * R, :] = y.astype(BF16)


def _pool(pool_in, lw):
    B, S, W = pool_in.shape
    return pl.pallas_call(
        functools.partial(_pool_kernel, S=S),
        out_shape=jax.ShapeDtypeStruct((B, S, W), BF16),
        grid=(B,),
        in_specs=[pl.BlockSpec((1, S, W), lambda b: (b, 0, 0)), _const_spec((W, W)), _const_spec((1, W))],
        out_specs=pl.BlockSpec((1, S, W), lambda b: (b, 0, 0)),
        scratch_shapes=[pltpu.VMEM((W // LANES, S + 2 * POOL_PAD, LANES), F32)],
        compiler_params=_params(("parallel",)),
        name="pool",
    )(pool_in, lw['w_pool'], lw['pool_scale'])


def _attention_kernel(q_ref, *refs, tq, T, tc):
    *ckr_refs, wk_ref, wv_ref, o_ref, kt_ref, v_ref, s_ref = refs
    nc = T // tc
    lane = lax.broadcasted_iota(jnp.int32, (1, LANES), 1)
    chunk_src = [(r, k) for r in ckr_refs for k in range(r.shape[1] // tc)]

    @pl.when(pl.program_id(1) == 0)
    def _():
        for c in range(nc):
            src, k = chunk_src[c]
            a = src[0, k * tc:(k + 1) * tc, :]
            kt = lax.dot_general(wk_ref[...], a, (((1,), (1,)), ((), ())), preferred_element_type=F32)
            kt_ref[:, c * tc:(c + 1) * tc] = kt.astype(BF16)
            v_ref[c * tc:(c + 1) * tc, :] = _dot(a[:, :KV_LORA], wv_ref[...]).astype(BF16)

    def scores(h, c, mrun):
        s = _dot(q_ref[0, :, h * HEAD_PAD:(h + 1) * HEAD_PAD],
                 kt_ref[h * HEAD_PAD:(h + 1) * HEAD_PAD, c * tc:(c + 1) * tc])
        s_ref[h % 2, :, c * tc:(c + 1) * tc] = s
        for g in range(tc // LANES):
            mrun = jnp.maximum(mrun, s[:, g * LANES:(g + 1) * LANES])
        return mrun

    def probs_pv(h, c, m, lrun, acc):
        p = jnp.exp2(s_ref[h % 2, :, c * tc:(c + 1) * tc] - m)
        for g in range(tc // LANES):
            lrun = lrun + p[:, g * LANES:(g + 1) * LANES]
        hp = h // 2
        acc = acc + _dot(p.astype(BF16), v_ref[c * tc:(c + 1) * tc, hp * LANES:(hp + 1) * LANES])
        return lrun, acc

    m_prev = None
    pair_out = None
    for slot in range(MLA_HEADS + 1):
        mrun = jnp.full((tq, LANES), -jnp.inf, F32)
        lrun = jnp.zeros((tq, LANES), F32)
        acc = jnp.zeros((tq, LANES), F32)
        for c in range(nc):
            if slot < MLA_HEADS:
                mrun = scores(slot, c, mrun)
            if slot >= 1:
                lrun, acc = probs_pv(slot - 1, c, m_prev, lrun, acc)
        if slot >= 1:
            h = slot - 1
            o = acc / jnp.sum(lrun, axis=-1, keepdims=True)
            if h % 2 == 0:
                pair_out = o
            else:
                hp = h // 2
                o_ref[0, :, hp * LANES:(hp + 1) * LANES] = jnp.where(lane < V_DIM, pair_out, o).astype(BF16)
        if slot < MLA_HEADS:
            m_prev = jnp.max(mrun, axis=-1, keepdims=True)


def _attention(q, ckr_parts, lw, *, tq, tc):
    B, S, _ = q.shape
    W = ckr_parts[0].shape[-1]
    T = sum(p.shape[1] for p in ckr_parts)
    assert all(p.shape[1] % tc == 0 for p in ckr_parts)
    return pl.pallas_call(
        functools.partial(_attention_kernel, tq=tq, T=T, tc=tc),
        out_shape=jax.ShapeDtypeStruct((B, S, MLA_HEADS * V_DIM), BF16),
        grid=(B, S // tq),
        in_specs=[pl.BlockSpec((1, tq, MLA_HEADS * HEAD_PAD), lambda b, i: (b, i, 0))]
                 + [pl.BlockSpec((1, p.shape[1], W), lambda b, i: (b, 0, 0)) for p in ckr_parts]
                 + [_const_spec((MLA_HEADS * HEAD_PAD, W)), _const_spec((KV_LORA, MLA_HEADS * V_DIM))],
        out_specs=pl.BlockSpec((1, tq, MLA_HEADS * V_DIM), lambda b, i: (b, i, 0)),
        scratch_shapes=[pltpu.VMEM((MLA_HEADS * HEAD_PAD, T), BF16), pltpu.VMEM((T, MLA_HEADS * V_DIM), BF16),
                        pltpu.VMEM((2, tq, T), F32)],
        compiler_params=_params(("arbitrary", "arbitrary")),
        name="attention",
    )(q, *ckr_parts, lw['wk_t'], lw['w_v'])


def _post_mixer_kernel(x_ref, attn_ref, pool_ref, sgu_ref, mod_ref, wout_ref, gpost_ref, gffn_ref, x1_ref, h2_ref,
                       *, tm):
    D = x_ref.shape[-1]
    half = tm // 2
    rb = half // (D // MXU_DIM)

    def mix(r0):
        rows = slice(r0, r0 + half)
        return jnp.concatenate([attn_ref[0, rows, :], pool_ref[0, rows, :], sgu_ref[0, rows, :]], axis=-1)

    def finish(out, r0):
        x1 = x_ref[0, r0:r0 + out.shape[0], :] + mod_ref[0, 2:3, :] * _rms(out, gpost_ref[...])
        x1_ref[0, r0:r0 + out.shape[0], :] = x1
        h2 = _rms(x1, gffn_ref[...]) * (1 + mod_ref[0, 4:5, :]) + mod_ref[0, 3:4, :]
        h2_ref[0, r0:r0 + out.shape[0], :] = h2.astype(BF16)

    out_a = _dot(mix(0), wout_ref[...])
    mix_b = mix(half)
    cols = []
    for j in range(D // MXU_DIM):
        cols.append(_dot(mix_b, wout_ref[:, j * MXU_DIM:(j + 1) * MXU_DIM]))
        finish(out_a[j * rb:(j + 1) * rb, :], j * rb)
    finish(jnp.concatenate(cols, axis=-1), half)


def _post_mixer(x, attn, pool, sgu, mod, lw, *, latent, tm):
    B, S, D = x.shape
    mod_idx = (lambda b, i: (b + 1, 0, 0)) if latent else (lambda b, i: (0, 0, 0))
    tok = lambda b, i: (b, i, 0)
    return pl.pallas_call(
        functools.partial(_post_mixer_kernel, tm=tm),
        out_shape=[jax.ShapeDtypeStruct((B, S, D), F32), jax.ShapeDtypeStruct((B, S, D), BF16)],
        grid=(B, S // tm),
        in_specs=[pl.BlockSpec((1, tm, D), tok),
                  pl.BlockSpec((1, tm, attn.shape[-1]), tok),
                  pl.BlockSpec((1, tm, POOL_W), tok),
                  pl.BlockSpec((1, tm, SGU_W), tok),
                  pl.BlockSpec((1, 8, D), mod_idx),
                  _layer_spec(lw['w_out'], lw['layer']),
                  _const_spec((1, D)),
                  _const_spec((1, D))],
        out_specs=[pl.BlockSpec((1, tm, D), tok), pl.BlockSpec((1, tm, D), tok)],
        compiler_params=_params(("parallel", "parallel")),
        name="post_mixer",
    )(x, attn, pool, sgu, mod, lw['w_out'], lw['g_post_mix'], lw['g_pre_ffn'])


def _interleave(a, b):
    out, ia, ib = [], 0, 0
    while ia < len(a) or ib < len(b):
        if ib >= len(b) or (ia < len(a) and ia * len(b) <= ib * len(a)):
            out.append(a[ia])
            ia += 1
        else:
            out.append(b[ib])
            ib += 1
    return out


def _conv_ffn_kernel(h_ref, hprev_ref, hnext_ref, x1_ref, mod_ref, wup_ref, cw_ref, wdown_ref, gpost_ref, o_ref,
                     hext_ref, zg0_ref, zv0_ref, zg1_ref, zv1_ref, a0_ref, a1_ref, acc_ref, *, tm, n_chunks, rb):
    i = pl.program_id(1)
    halo = BF16_SUBLANES
    D = hext_ref.shape[1]
    zero = jnp.zeros((), BF16)
    hext_ref[0:halo, :] = jnp.where(i > 0, hprev_ref[0], zero)
    hext_ref[halo:halo + tm, :] = h_ref[0]
    hext_ref[halo + tm:halo + tm + halo, :] = jnp.where(i < pl.num_programs(1) - 1, hnext_ref[0], zero)
    acc_ref[...] = jnp.zeros_like(acc_ref)
    zs = ((zg0_ref, zv0_ref), (zg1_ref, zv1_ref))
    as_ = (a0_ref, a1_ref)

    def conv_pair(z_ref, slab, cw, r0):
        n = rb // 2
        base = halo + r0
        e_m = z_ref[slab, pl.ds(base - 1, n, stride=2), :]
        e_c = z_ref[slab, pl.ds(base, n, stride=2), :]
        e_p = z_ref[slab, pl.ds(base + 1, n, stride=2), :]
        e_pp = z_ref[slab, pl.ds(base + 2, n, stride=2), :]
        w0, w1, w2, b = cw[0:1, :], cw[1:2, :], cw[2:3, :], cw[3:4, :]
        even = e_m * w0 + e_c * w1 + e_p * w2 + b
        odd = e_c * w0 + e_p * w1 + e_pp * w2 + b
        return even, odd

    def up_pieces(c, z_pair):
        pieces = []
        for which in range(2):
            state = {}

            def piece(k, which=which, state=state):
                col = (which * n_chunks + c) * FF_CHUNK
                part = _dot(hext_ref[:, k * MXU_DIM:(k + 1) * MXU_DIM],
                            wup_ref[k * MXU_DIM:(k + 1) * MXU_DIM, col:col + FF_CHUNK])
                state['z'] = part if k == 0 else state['z'] + part
                if k == D // MXU_DIM - 1:
                    for slab in range(FF_CHUNK // LANES):
                        z_pair[which][slab] = state['z'][:, slab * LANES:(slab + 1) * LANES]

            pieces += [functools.partial(piece, k) for k in range(D // MXU_DIM)]
        return pieces

    def gate_pieces(c, z_pair, a_ref):
        def piece(r0):
            for slab in range(FF_CHUNK // LANES):
                ls = slice(slab * LANES, (slab + 1) * LANES)
                col = c * FF_CHUNK + slab * LANES
                cwg = cw_ref[:, col:col + LANES]
                cwv = cw_ref[:, n_chunks * FF_CHUNK + col:n_chunks * FF_CHUNK + col + LANES]
                g_e, g_o = conv_pair(z_pair[0], slab, cwg, r0)
                v_e, v_o = conv_pair(z_pair[1], slab, cwv, r0)
                a_ref[r0 // 2:(r0 + rb) // 2, ls] = pltpu.pack_elementwise(
                    [jax.nn.silu(g_e) * v_e, jax.nn.silu(g_o) * v_o], packed_dtype=BF16)

        return [functools.partial(piece, r * rb) for r in range(tm // rb)]

    def down_pieces(c, a_ref):
        def piece(n):
            ns = slice(n * MXU_DIM, (n + 1) * MXU_DIM)
            acc_ref[:, ns] += _dot(pltpu.bitcast(a_ref[...], BF16), wdown_ref[c * FF_CHUNK:(c + 1) * FF_CHUNK, ns])

        return [functools.partial(piece, n) for n in range(D // MXU_DIM)]

    for s in range(n_chunks + 2):
        par = s % 2
        mxu, vpu = [], []
        if s < n_chunks:
            mxu += up_pieces(s, zs[par])
        if 1 <= s <= n_chunks:
            vpu += gate_pieces(s - 1, zs[1 - par], as_[1 - par])
        if s >= 2:
            mxu += down_pieces(s - 2, as_[par])
        for emit in _interleave(mxu, vpu):
            emit()
    o_ref[0] = x1_ref[0] + mod_ref[0, 5:6, :] * _rms(acc_ref[...], gpost_ref[...])


def _conv_ffn(h2, x1, mod, lw, *, latent, tm):
    B, S, D = x1.shape
    n_chunks = lw['w_down'].shape[1] // FF_CHUNK
    halo = BF16_SUBLANES
    nblk = tm // halo
    last = S // halo - 1
    mod_idx = (lambda b, i: (b + 1, 0, 0)) if latent else (lambda b, i: (0, 0, 0))
    tok = lambda b, i: (b, i, 0)
    z_scratch = pltpu.VMEM((FF_CHUNK // LANES, tm + 2 * halo, LANES), F32)
    a_scratch = pltpu.VMEM((tm // 2, FF_CHUNK), jnp.uint32)
    return pl.pallas_call(
        functools.partial(_conv_ffn_kernel, tm=tm, n_chunks=n_chunks, rb=FF_GATE_ROWS),
        out_shape=jax.ShapeDtypeStruct((B, S, D), F32),
        grid=(B, S // tm),
        in_specs=[pl.BlockSpec((1, tm, D), tok),
                  pl.BlockSpec((1, halo, D), lambda b, i: (b, jnp.maximum(i * nblk - 1, 0), 0)),
                  pl.BlockSpec((1, halo, D), lambda b, i: (b, jnp.minimum((i + 1) * nblk, last), 0)),
                  pl.BlockSpec((1, tm, D), tok),
                  pl.BlockSpec((1, 8, D), mod_idx),
                  _layer_spec(lw['w_up'], lw['layer']),
                  _const_spec(lw['conv'].shape),
                  _layer_spec(lw['w_down'], lw['layer']),
                  _const_spec((1, D))],
        out_specs=pl.BlockSpec((1, tm, D), tok),
        scratch_shapes=[pltpu.VMEM((tm + 2 * halo, D), BF16), z_scratch, z_scratch, z_scratch, z_scratch,
                        a_scratch, a_scratch, pltpu.VMEM((tm, D), F32)],
        compiler_params=_params(("parallel", "parallel")),
        name="conv_ffn",
    )(h2, h2, h2, x1, mod, lw['w_up'], lw['conv'], lw['w_down'], lw['g_post_ffn'])


def _rope_tables(S):
    n_rows = S // GRID_W
    rows = jnp.repeat(jnp.arange(n_rows, dtype=F32), GRID_W)
    cols = jnp.tile(jnp.arange(GRID_W, dtype=F32), n_rows)
    freqs = ROPE_BASE ** (-jnp.arange(ROPE_FREQS, dtype=F32) / ROPE_FREQS)
    ang = jnp.stack([rows[:, None] * freqs, cols[:, None] * freqs], axis=1)
    cos, sin = jnp.cos(ang), jnp.sin(ang)
    zero = jnp.zeros_like(sin)
    c = jnp.stack([cos, cos], axis=2).reshape(S, ROPE_DIM)
    s_up = jnp.stack([-sin, zero], axis=2).reshape(S, ROPE_DIM)
    s_dn = jnp.stack([zero, sin], axis=2).reshape(S, ROPE_DIM)

    def widen(t, lead, lead_value, scale):
        pads = [jnp.full((S, lead), lead_value, F32), t * scale, jnp.zeros((S, LANES - lead - ROPE_DIM), F32)]
        return jnp.concatenate(pads, axis=1)

    q_tabs = (widen(c, NOPE_DIM, Q_SCALE, Q_SCALE), widen(s_up + s_dn, NOPE_DIM, 0.0, Q_SCALE))
    k_tabs = (widen(c, 0, 0.0, 1.0), widen(s_up, 0, 0.0, 1.0), widen(s_dn, 0, 0.0, 1.0))
    return q_tabs + k_tabs


def _layer_weights(l, g_pre_mix, g_post_mix, g_pre_ffn, g_post_ffn, w_in, g_q_a, w_q_b, g_kv_a, w_kv_b,
                   w_pool, pool_scale, g_sgu, w_sgu, b_sgu, w_out, w_up, conv_w, conv_b, w_down):
    D = w_in.shape[1]
    o1 = Q_LORA
    o2 = o1 + KV_LORA
    o3 = o2 + ROPE_DIM
    o4 = o3 + POOL_W
    wi = w_in[l]
    w_in_r = jnp.concatenate([wi[:, :o2], wi[:, o3:], wi[:, o2:o3], jnp.zeros((D, LANES - ROPE_DIM), F32)], axis=1)

    qd = NOPE_DIM + ROPE_DIM
    wq = w_q_b[l].reshape(Q_LORA, MLA_HEADS, qd)
    hz = jnp.zeros((Q_LORA, MLA_HEADS, HEAD_PAD - qd), F32)
    wq_pad = jnp.concatenate([wq, hz], axis=-1).reshape(Q_LORA, MLA_HEADS * HEAD_PAD)
    wq_rope = wq[..., NOPE_DIM:].reshape(Q_LORA, MLA_HEADS, 2, 2, ROPE_FREQS)
    wq_partner = wq_rope[..., ::-1, :].reshape(Q_LORA, MLA_HEADS, ROPE_DIM)
    wq_sw = jnp.concatenate([jnp.zeros((Q_LORA, MLA_HEADS, NOPE_DIM), F32), wq_partner, hz],
                            axis=-1).reshape(Q_LORA, MLA_HEADS * HEAD_PAD)

    wkv = w_kv_b[l].reshape(KV_LORA, MLA_HEADS, NOPE_DIM + V_DIM)
    wk_t = jnp.transpose(wkv[..., :NOPE_DIM], (1, 2, 0))
    wk_full = jnp.zeros((MLA_HEADS, HEAD_PAD, KV_LORA + ROPE_DIM), F32)
    wk_full = wk_full.at[:, :NOPE_DIM, :KV_LORA].set(wk_t)
    wk_full = wk_full.at[:, NOPE_DIM:qd, KV_LORA:].set(jnp.eye(ROPE_DIM, dtype=F32))
    w_v = wkv[..., NOPE_DIM:].reshape(KV_LORA, MLA_HEADS * V_DIM)

    wp = jnp.zeros((POOL_W, POOL_W), F32)
    for g in range(len(POOL_WINDOWS)):
        wp = wp.at[g * POOL_GC:(g + 1) * POOL_GC, g * POOL_GC:(g + 1) * POOL_GC].set(w_pool[l, g])

    d_ff = w_down.shape[1]
    conv = jnp.concatenate([conv_w[l], conv_b[l][None, :], jnp.zeros((4, 2 * d_ff), F32)], axis=0)
    row = lambda a: a[l][None, :]
    return {
        'layer': l,
        'g_pre_mix': row(g_pre_mix), 'g_post_mix': row(g_post_mix),
        'g_pre_ffn': row(g_pre_ffn), 'g_post_ffn': row(g_post_ffn),
        'w_in': w_in_r.astype(BF16), 'g_q_a': row(g_q_a),
        'w_q_b': jnp.concatenate([wq_pad, wq_sw], axis=1).astype(BF16),
        'g_kv_a': row(g_kv_a), 'wk_t': wk_full.reshape(MLA_HEADS * HEAD_PAD, -1).astype(BF16), 'w_v': w_v.astype(BF16),
        'w_pool': wp.astype(BF16), 'pool_scale': row(pool_scale),
        'g_sgu': row(g_sgu), 'w_sgu': w_sgu[l].astype(BF16),
        'b_sgu': jnp.repeat(b_sgu[l].T, SGU_HD, axis=1),
        'w_out': w_out, 'w_up': w_up, 'w_down': w_down, 'conv': conv,
    }


def _layer(x, mod, lw, tabs, ctx_ckr, *, latent, tm, tq):
    B, S, _ = x.shape
    fold = 1 if latent else max(1, min(B, TOKEN_TILE // S))
    folded = lambda a: a.reshape(B // fold, S * fold, a.shape[-1])
    unfolded = lambda a: a.reshape(B, S, a.shape[-1])
    q, ckr, pool_in, sgu, *cache = map(unfolded, _pre_mixer(folded(x), mod, lw, tabs, latent=latent, tm=tm * fold))
    ckr_parts = [ctx_ckr.astype(BF16), ckr] if latent else [ckr]
    attn = _attention(q, ckr_parts, lw, tq=tq, tc=min(ATTN_KEYS, *[p.shape[1] for p in ckr_parts]))
    pool = _pool(pool_in, lw)
    x1, h2 = map(unfolded, _post_mixer(*map(folded, (x, attn, pool, sgu)), mod, lw, latent=latent, tm=tm * fold))
    return _conv_ffn(h2, x1, mod, lw, latent=latent, tm=tm), cache


def kernel(x_prompt, x_sample, cache_ckv, cache_krope, c, c_ctx, w_mod, b_mod, g_pre_mix, g_post_mix, g_pre_ffn, g_post_ffn, w_in, g_q_a, w_q_b, g_kv_a, w_kv_b, w_pool, pool_scale, g_sgu, w_sgu, b_sgu, w_out, w_up, conv_w, conv_b, w_down):
    depth = w_in.shape[0]
    D = x_prompt.shape[-1]
    n_lat = c.shape[0]
    mod_rows = 16
    cc = jnp.concatenate([c_ctx[None, :], c, jnp.zeros((mod_rows - 1 - n_lat, D), F32)], axis=0)
    mod_all = _modulation(cc, w_mod, b_mod)
    mod_all = jnp.pad(mod_all.reshape(depth, mod_rows, 6, D), ((0, 0), (0, 0), (0, 2), (0, 0)))
    tabs = _rope_tables(x_sample.shape[1])

    xp, xs = x_prompt, x_sample
    ckv_list, kr_list = [], []
    big = [w.astype(BF16) for w in (w_out, w_up, w_down)]
    for l in range(depth):
        lw = _layer_weights(l, g_pre_mix, g_post_mix, g_pre_ffn, g_post_ffn, w_in, g_q_a, w_q_b, g_kv_a, w_kv_b,
                            w_pool, pool_scale, g_sgu, w_sgu, b_sgu, big[0], big[1], conv_w, conv_b, big[2])
        xp, (ckv, kr) = _layer(xp, mod_all[l], lw, None, None, latent=False, tm=xp.shape[1], tq=xp.shape[1])
        ckv_list.append(ckv)
        kr_list.append(kr)
        ctx_ckr = jnp.concatenate([cache_ckv[:, l], cache_krope[:, l]], axis=-1)
        xs, _ = _layer(xs, mod_all[l], lw, tabs, ctx_ckr, latent=True, tm=TOKEN_TILE, tq=ATTN_QUERIES)
    return (xp, xs, jnp.stack(ckv_list, axis=1), jnp.stack(kr_list, axis=1))
```

```python
import functools

import jax
import jax.numpy as jnp
from jax import lax
from jax.experimental import pallas as pl
from jax.experimental.pallas import tpu as pltpu

F32 = jnp.float32
BF16 = jnp.bfloat16

GRID_W = 64
MLA_HEADS = 8
NOPE_DIM = 64
ROPE_DIM = 32
ROPE_FREQS = ROPE_DIM // 4
V_DIM = 64
Q_LORA = 256
KV_LORA = 128
ROPE_BASE = 10000.0
SM_SCALE = (NOPE_DIM + ROPE_DIM) ** -0.5
LOG2E = 1.4426950408889634
Q_SCALE = SM_SCALE * LOG2E
POOL_WINDOWS = (2, 4, 8, 16)
POOL_GC = 64
POOL_W = POOL_GC * len(POOL_WINDOWS)
SGU_HEADS = 4
SGU_HD = 64
SGU_W = SGU_HEADS * SGU_HD
CHUNK = 128
EPS = 1e-6

LANES = 128
BF16_SUBLANES = 16
HEAD_PAD = LANES
VMEM_LIMIT = 62 * 1024 * 1024

C_QA = 0
C_KVA = C_QA + Q_LORA
C_POOL = C_KVA + KV_LORA
C_U = C_POOL + POOL_W
C_V = C_U + SGU_W
C_KR = C_V + SGU_W
IN_COLS_PAD = C_KR + LANES

FF_CHUNK = 256
FF_GATE_ROWS = 64
MXU_DIM = 256
ATTN_KEYS = 512
ATTN_QUERIES = 256
TOKEN_TILE = 512
POOL_PAD = 16
POOL_ROWS = 256
MOD_COLS = 1536
F32_SUBLANES = 8


def _rms(x, g):
    return x * lax.rsqrt(jnp.mean(x * x, axis=-1, keepdims=True) + EPS) * g


def _dot(a, b):
    return jnp.dot(a, b, preferred_element_type=F32)


def _params(sem):
    return pltpu.CompilerParams(dimension_semantics=sem, vmem_limit_bytes=VMEM_LIMIT)


def _layer_spec(w, l):
    nd = w.ndim - 1
    return pl.BlockSpec((None,) + w.shape[1:], lambda *_: (l,) + (0,) * nd)


def _mod_kernel(c_ref, w_ref, b_ref, o_ref):
    a = jax.nn.silu(c_ref[...]).astype(BF16)
    o_ref[0] = _dot(a, w_ref[0].astype(BF16)) + b_ref[0]


def _modulation(cc, w_mod, b_mod):
    depth, d, e = w_mod.shape
    rows = cc.shape[0]
    te = MOD_COLS
    return pl.pallas_call(
        _mod_kernel,
        out_shape=jax.ShapeDtypeStruct((depth, rows, e), F32),
        grid=(depth, e // te),
        in_specs=[
            pl.BlockSpec((rows, d), lambda l, j: (0, 0)),
            pl.BlockSpec((1, d, te), lambda l, j: (l, 0, j)),
            pl.BlockSpec((1, 1, te), lambda l, j: (l, 0, j)),
        ],
        out_specs=pl.BlockSpec((1, rows, te), lambda l, j: (l, 0, j)),
        compiler_params=_params(("parallel", "parallel")),
        name="modulation",
    )(cc, w_mod, b_mod.reshape(depth, 1, e))


def _rope(x, tabs):
    c_ref, up_ref, dn_ref = tabs
    up = pltpu.roll(x, LANES - ROPE_FREQS, axis=1)
    dn = pltpu.roll(x, ROPE_FREQS, axis=1)
    return x * c_ref[...] + up * up_ref[...] + dn * dn_ref[...]


def _pre_mixer_kernel(*refs, latent, tm):
    if latent:
        (x_ref, mod_ref, gpre_ref, win_ref, gqa_ref, wqb_ref, gkva_ref, gsgu_ref, wsgu_ref, bsgu_ref,
         *tabs, q_ref, ckr_ref, pool_ref, sgu_ref) = refs
    else:
        (x_ref, mod_ref, gpre_ref, win_ref, gqa_ref, wqb_ref, gkva_ref, gsgu_ref, wsgu_ref, bsgu_ref,
         q_ref, ckr_ref, pool_ref, sgu_ref, ckv_ref, kr_ref) = refs
    x = x_ref[0]
    shift = mod_ref[0, 0:1, :]
    scale = mod_ref[0, 1:2, :]
    h = _rms(x, gpre_ref[...]) * (1 + scale) + shift
    proj = _dot(h.astype(BF16), win_ref[...])

    qn = _rms(proj[:, C_QA:C_QA + Q_LORA], gqa_ref[...]).astype(BF16)
    if latent:
        qq = _dot(qn, wqb_ref[...])
    else:
        qq = _dot(qn, wqb_ref[:, 0:MLA_HEADS * HEAD_PAD])
    width = MLA_HEADS * HEAD_PAD
    for hh in range(MLA_HEADS):
        lo = hh * HEAD_PAD
        if latent:
            qh = qq[:, lo:lo + HEAD_PAD] * tabs[0][...] + qq[:, width + lo:width + lo + HEAD_PAD] * tabs[1][...]
        else:
            qh = qq[:, lo:lo + HEAD_PAD] * Q_SCALE
        q_ref[0, :, lo:lo + HEAD_PAD] = qh.astype(BF16)

    ckv = _rms(proj[:, C_KVA:C_KVA + KV_LORA], gkva_ref[...])
    kr = proj[:, C_KR:C_KR + LANES]
    if latent:
        kr = _rope(kr, tabs[2:])
    kr = kr[:, :ROPE_DIM]
    if not latent:
        ckv_ref[0] = ckv
        kr_ref[0] = kr
    ckr_ref[0, :, 0:KV_LORA] = ckv.astype(BF16)
    ckr_ref[0, :, KV_LORA:KV_LORA + ROPE_DIM] = kr.astype(BF16)
    pool_ref[0] = proj[:, C_POOL:C_POOL + POOL_W]

    u = proj[:, C_U:C_U + SGU_W]
    vn = _rms(proj[:, C_V:C_V + SGU_W], gsgu_ref[...]).astype(BF16)
    lane = lax.broadcasted_iota(jnp.int32, (1, LANES), 1)
    zero = jnp.zeros((), BF16)
    for c in range(tm // CHUNK):
        r0 = c * CHUNK
        for pair in range(SGU_W // LANES):
            l0 = pair * LANES
            vp = vn[r0:r0 + CHUNK, l0:l0 + LANES]
            v_lo = jnp.where(lane < SGU_HD, vp, zero)
            v_hi = jnp.where(lane >= SGU_HD, vp, zero)
            z = _dot(wsgu_ref[2 * pair], v_lo) + _dot(wsgu_ref[2 * pair + 1], v_hi) + bsgu_ref[:, l0:l0 + LANES]
            sgu_ref[0, r0:r0 + CHUNK, l0:l0 + LANES] = (u[r0:r0 + CHUNK, l0:l0 + LANES] * z).astype(BF16)


def _pre_mixer(x, mod, lw, tabs, *, latent, tm):
    B, S, D = x.shape
    mod_idx = (lambda i, b: (b + 1, 0, 0)) if latent else (lambda i, b: (0, 0, 0))
    tok = lambda i, b: (b, i, 0)
    names = ['g_pre_mix', 'w_in', 'g_q_a', 'w_q_b', 'g_kv_a', 'g_sgu', 'w_sgu', 'b_sgu']
    in_specs = [pl.BlockSpec((1, tm, D), tok), pl.BlockSpec((1, F32_SUBLANES, D), mod_idx)]
    in_specs += [_layer_spec(lw[n], lw['layer']) for n in names]
    args = [x, mod] + [lw[n] for n in names]
    if latent:
        in_specs += [pl.BlockSpec((tm, LANES), lambda i, b: (i, 0)) for _ in tabs]
        args += list(tabs)
    out_shape = [
        jax.ShapeDtypeStruct((B, S, MLA_HEADS * HEAD_PAD), BF16),
        jax.ShapeDtypeStruct((B, S, KV_LORA + ROPE_DIM), BF16),
        jax.ShapeDtypeStruct((B, S, POOL_W), F32),
        jax.ShapeDtypeStruct((B, S, SGU_W), BF16),
    ]
    out_specs = [
        pl.BlockSpec((1, tm, MLA_HEADS * HEAD_PAD), tok),
        pl.BlockSpec((1, tm, KV_LORA + ROPE_DIM), tok),
        pl.BlockSpec((1, tm, POOL_W), tok),
        pl.BlockSpec((1, tm, SGU_W), tok),
    ]
    if not latent:
        out_shape += [jax.ShapeDtypeStruct((B, S, KV_LORA), F32), jax.ShapeDtypeStruct((B, S, ROPE_DIM), F32)]
        out_specs += [pl.BlockSpec((1, tm, KV_LORA), tok), pl.BlockSpec((1, tm, ROPE_DIM), tok)]
    return pl.pallas_call(
        functools.partial(_pre_mixer_kernel, latent=latent, tm=tm),
        out_shape=out_shape,
        grid=(S // tm, B),
        in_specs=in_specs,
        out_specs=out_specs,
        compiler_params=_params(("parallel", "parallel")),
        name="pre_mixer_lat" if latent else "pre_mixer_ctx",
    )(*args)


def _pool_kernel(x_ref, w_ref, scale_ref, o_ref, pad_ref, *, S):
    nslab = POOL_W // LANES
    zeros = jnp.zeros((POOL_PAD, LANES), F32)
    for slab in range(nslab):
        pad_ref[slab, 0:POOL_PAD, :] = zeros
        pad_ref[slab, POOL_PAD + S:POOL_PAD + S + POOL_PAD, :] = zeros
        pad_ref[slab, POOL_PAD:POOL_PAD + S, :] = x_ref[0, :, slab * LANES:(slab + 1) * LANES]

    R = min(POOL_ROWS, S)
    n = R // 2
    lane = lax.broadcasted_iota(jnp.int32, (1, LANES), 1)
    lane_w = lax.broadcasted_iota(jnp.int32, (1, POOL_W), 1)
    half = jnp.where(lane_w < POOL_GC, POOL_WINDOWS[0] // 2,
                     jnp.where(lane_w < 2 * POOL_GC, POOL_WINDOWS[1] // 2,
                               jnp.where(lane_w < 3 * POOL_GC, POOL_WINDOWS[2] // 2, POOL_WINDOWS[3] // 2)))
    for r in range(S // R):
        base = POOL_PAD + r * R

        def window_sums(slab, lo_small, lo_big, par):
            def ld(off):
                return pad_ref[slab, pl.ds(base + par + off, n, stride=2), :]

            x = ld(0)
            small = x
            for off in range(-lo_small, lo_small):
                if off != 0:
                    small = small + ld(off)
            big = small
            for off in list(range(-lo_big, -lo_small)) + list(range(lo_small, lo_big)):
                big = big + ld(off)
            return x, jnp.where(lane < POOL_GC, small, big)

        ds = []
        for par in range(2):
            xa, tot_a = window_sums(0, POOL_WINDOWS[0] // 2, POOL_WINDOWS[1] // 2, par)
            xb, tot_b = window_sums(1, POOL_WINDOWS[2] // 2, POOL_WINDOWS[3] // 2, par)
            t = r * R + par + 2 * lax.broadcasted_iota(jnp.int32, (n, 1), 0)
            cnt = (jnp.minimum(t + half, S) - jnp.maximum(t - half, 0)).astype(F32)
            ds.append(jnp.concatenate([tot_a, tot_b], axis=1) / cnt - jnp.concatenate([xa, xb], axis=1))
        d = pltpu.bitcast(pltpu.pack_elementwise(ds, packed_dtype=BF16), BF16)
        y = _dot(d, w_ref[...]) * scale_ref[...]
        o_ref[0, r * R:(r + 1) * R, :] = y.astype(BF16)


def _pool(pool_in, lw):
    B, S, W = pool_in.shape
    return pl.pallas_call(
        functools.partial(_pool_kernel, S=S),
        out_shape=jax.ShapeDtypeStruct((B, S, W), BF16),
        grid=(B,),
        in_specs=[pl.BlockSpec((1, S, W), lambda b: (b, 0, 0)), _layer_spec(lw['w_pool'], lw['layer']),
                  _layer_spec(lw['pool_scale'], lw['layer'])],
        out_specs=pl.BlockSpec((1, S, W), lambda b: (b, 0, 0)),
        scratch_shapes=[pltpu.VMEM((W // LANES, S + 2 * POOL_PAD, LANES), F32)],
        compiler_params=_params(("parallel",)),
        name="pool",
    )(pool_in, lw['w_pool'], lw['pool_scale'])


def _attention_kernel(q_ref, *refs, tq, T, tc):
    *ckr_refs, wk_ref, wv_ref, o_ref, kt_ref, v_ref, s_ref = refs
    nc = T // tc
    lane = lax.broadcasted_iota(jnp.int32, (1, LANES), 1)
    chunk_src = [(r, k) for r in ckr_refs for k in range(r.shape[1] // tc)]

    @pl.when(pl.program_id(1) == 0)
    def _():
        for c in range(nc):
            src, k = chunk_src[c]
            a = src[0, k * tc:(k + 1) * tc, :]
            kt = lax.dot_general(wk_ref[...], a, (((1,), (1,)), ((), ())), preferred_element_type=F32)
            kt_ref[:, c * tc:(c + 1) * tc] = kt.astype(BF16)
            v_ref[c * tc:(c + 1) * tc, :] = _dot(a[:, :KV_LORA], wv_ref[...]).astype(BF16)

    def scores(h, c, mrun):
        s = _dot(q_ref[0, :, h * HEAD_PAD:(h + 1) * HEAD_PAD],
                 kt_ref[h * HEAD_PAD:(h + 1) * HEAD_PAD, c * tc:(c + 1) * tc])
        s_ref[h % 2, :, c * tc:(c + 1) * tc] = s
        for g in range(tc // LANES):
            mrun = jnp.maximum(mrun, s[:, g * LANES:(g + 1) * LANES])
        return mrun

    def probs_pv(h, c, m, lrun, acc):
        p = jnp.exp2(s_ref[h % 2, :, c * tc:(c + 1) * tc] - m)
        for g in range(tc // LANES):
            lrun = lrun + p[:, g * LANES:(g + 1) * LANES]
        hp = h // 2
        acc = acc + _dot(p.astype(BF16), v_ref[c * tc:(c + 1) * tc, hp * LANES:(hp + 1) * LANES])
        return lrun, acc

    m_prev = None
    pair_out = None
    for slot in range(MLA_HEADS + 1):
        mrun = jnp.full((tq, LANES), -jnp.inf, F32)
        lrun = jnp.zeros((tq, LANES), F32)
        acc = jnp.zeros((tq, LANES), F32)
        for c in range(nc):
            if slot < MLA_HEADS:
                mrun = scores(slot, c, mrun)
            if slot >= 1:
                lrun, acc = probs_pv(slot - 1, c, m_prev, lrun, acc)
        if slot >= 1:
            h = slot - 1
            o = acc / jnp.sum(lrun, axis=-1, keepdims=True)
            if h % 2 == 0:
                pair_out = o
            else:
                hp = h // 2
                o_ref[0, :, hp * LANES:(hp + 1) * LANES] = jnp.where(lane < V_DIM, pair_out, o).astype(BF16)
        if slot < MLA_HEADS:
            m_prev = jnp.max(mrun, axis=-1, keepdims=True)


def _attention(q, ckr_parts, lw, *, tq, tc):
    B, S, _ = q.shape
    W = ckr_parts[0].shape[-1]
    T = sum(p.shape[1] for p in ckr_parts)
    assert all(p.shape[1] % tc == 0 for p in ckr_parts)
    return pl.pallas_call(
        functools.partial(_attention_kernel, tq=tq, T=T, tc=tc),
        out_shape=jax.ShapeDtypeStruct((B, S, MLA_HEADS * V_DIM), BF16),
        grid=(B, S // tq),
        in_specs=[pl.BlockSpec((1, tq, MLA_HEADS * HEAD_PAD), lambda b, i: (b, i, 0))]
                 + [pl.BlockSpec((1, p.shape[1], W), lambda b, i: (b, 0, 0)) for p in ckr_parts]
                 + [_layer_spec(lw['wk_t'], lw['layer']), _layer_spec(lw['w_v'], lw['layer'])],
        out_specs=pl.BlockSpec((1, tq, MLA_HEADS * V_DIM), lambda b, i: (b, i, 0)),
        scratch_shapes=[pltpu.VMEM((MLA_HEADS * HEAD_PAD, T), BF16), pltpu.VMEM((T, MLA_HEADS * V_DIM), BF16),
                        pltpu.VMEM((2, tq, T), F32)],
        compiler_params=_params(("arbitrary", "arbitrary")),
        name="attention",
    )(q, *ckr_parts, lw['wk_t'], lw['w_v'])


def _post_mixer_kernel(x_ref, attn_ref, pool_ref, sgu_ref, mod_ref, wout_ref, gpost_ref, gffn_ref, x1_ref, h2_ref,
                       *, tm):
    D = x_ref.shape[-1]
    half = tm // 2
    rb = half // (D // MXU_DIM)

    def mix(r0):
        rows = slice(r0, r0 + half)
        return jnp.concatenate([attn_ref[0, rows, :], pool_ref[0, rows, :], sgu_ref[0, rows, :]], axis=-1)

    def finish(out, r0):
        x1 = x_ref[0, r0:r0 + out.shape[0], :] + mod_ref[0, 2:3, :] * _rms(out, gpost_ref[...])
        x1_ref[0, r0:r0 + out.shape[0], :] = x1
        h2 = _rms(x1, gffn_ref[...]) * (1 + mod_ref[0, 4:5, :]) + mod_ref[0, 3:4, :]
        h2_ref[0, r0:r0 + out.shape[0], :] = h2.astype(BF16)

    out_a = _dot(mix(0), wout_ref[...])
    mix_b = mix(half)
    cols = []
    for j in range(D // MXU_DIM):
        cols.append(_dot(mix_b, wout_ref[:, j * MXU_DIM:(j + 1) * MXU_DIM]))
        finish(out_a[j * rb:(j + 1) * rb, :], j * rb)
    finish(jnp.concatenate(cols, axis=-1), half)


def _post_mixer(x, attn, pool, sgu, mod, lw, *, latent, tm):
    B, S, D = x.shape
    mod_idx = (lambda b, i: (b + 1, 0, 0)) if latent else (lambda b, i: (0, 0, 0))
    tok = lambda b, i: (b, i, 0)
    return pl.pallas_call(
        functools.partial(_post_mixer_kernel, tm=tm),
        out_shape=[jax.ShapeDtypeStruct((B, S, D), F32), jax.ShapeDtypeStruct((B, S, D), BF16)],
        grid=(B, S // tm),
        in_specs=[pl.BlockSpec((1, tm, D), tok),
                  pl.BlockSpec((1, tm, attn.shape[-1]), tok),
                  pl.BlockSpec((1, tm, POOL_W), tok),
                  pl.BlockSpec((1, tm, SGU_W), tok),
                  pl.BlockSpec((1, F32_SUBLANES, D), mod_idx),
                  _layer_spec(lw['w_out'], lw['layer']),
                  _layer_spec(lw['g_post_mix'], lw['layer']),
                  _layer_spec(lw['g_pre_ffn'], lw['layer'])],
        out_specs=[pl.BlockSpec((1, tm, D), tok), pl.BlockSpec((1, tm, D), tok)],
        compiler_params=_params(("parallel", "parallel")),
        name="post_mixer",
    )(x, attn, pool, sgu, mod, lw['w_out'], lw['g_post_mix'], lw['g_pre_ffn'])


def _interleave(a, b):
    out, ia, ib = [], 0, 0
    while ia < len(a) or ib < len(b):
        if ib >= len(b) or (ia < len(a) and ia * len(b) <= ib * len(a)):
            out.append(a[ia])
            ia += 1
        else:
            out.append(b[ib])
            ib += 1
    return out


def _conv_ffn_kernel(h_ref, hprev_ref, hnext_ref, x1_ref, mod_ref, wup_ref, cw_ref, wdown_ref, gpost_ref, o_ref,
                     hext_ref, zg0_ref, zv0_ref, zg1_ref, zv1_ref, a0_ref, a1_ref, acc_ref, *, tm, n_chunks, rb):
    i = pl.program_id(1)
    halo = BF16_SUBLANES
    D = hext_ref.shape[1]
    zero = jnp.zeros((), BF16)
    hext_ref[0:halo, :] = jnp.where(i > 0, hprev_ref[0], zero)
    hext_ref[halo:halo + tm, :] = h_ref[0]
    hext_ref[halo + tm:halo + tm + halo, :] = jnp.where(i < pl.num_programs(1) - 1, hnext_ref[0], zero)
    acc_ref[...] = jnp.zeros_like(acc_ref)
    zs = ((zg0_ref, zv0_ref), (zg1_ref, zv1_ref))
    as_ = (a0_ref, a1_ref)

    def conv_pair(z_ref, slab, cw, r0):
        n = rb // 2
        base = halo + r0
        e_m = z_ref[slab, pl.ds(base - 1, n, stride=2), :]
        e_c = z_ref[slab, pl.ds(base, n, stride=2), :]
        e_p = z_ref[slab, pl.ds(base + 1, n, stride=2), :]
        e_pp = z_ref[slab, pl.ds(base + 2, n, stride=2), :]
        w0, w1, w2, b = cw[0:1, :], cw[1:2, :], cw[2:3, :], cw[3:4, :]
        even = e_m * w0 + e_c * w1 + e_p * w2 + b
        odd = e_c * w0 + e_p * w1 + e_pp * w2 + b
        return even, odd

    def up_pieces(c, z_pair):
        pieces = []
        for which in range(2):
            state = {}

            def piece(k, which=which, state=state):
                col = (which * n_chunks + c) * FF_CHUNK
                part = _dot(hext_ref[:, k * MXU_DIM:(k + 1) * MXU_DIM],
                            wup_ref[k * MXU_DIM:(k + 1) * MXU_DIM, col:col + FF_CHUNK])
                state['z'] = part if k == 0 else state['z'] + part
                if k == D // MXU_DIM - 1:
                    for slab in range(FF_CHUNK // LANES):
                        z_pair[which][slab] = state['z'][:, slab * LANES:(slab + 1) * LANES]

            pieces += [functools.partial(piece, k) for k in range(D // MXU_DIM)]
        return pieces

    def gate_pieces(c, z_pair, a_ref):
        def piece(r0):
            for slab in range(FF_CHUNK // LANES):
                ls = slice(slab * LANES, (slab + 1) * LANES)
                col = c * FF_CHUNK + slab * LANES
                cwg = cw_ref[:, col:col + LANES]
                cwv = cw_ref[:, n_chunks * FF_CHUNK + col:n_chunks * FF_CHUNK + col + LANES]
                g_e, g_o = conv_pair(z_pair[0], slab, cwg, r0)
                v_e, v_o = conv_pair(z_pair[1], slab, cwv, r0)
                a_ref[r0 // 2:(r0 + rb) // 2, ls] = pltpu.pack_elementwise(
                    [jax.nn.silu(g_e) * v_e, jax.nn.silu(g_o) * v_o], packed_dtype=BF16)

        return [functools.partial(piece, r * rb) for r in range(tm // rb)]

    def down_pieces(c, a_ref):
        def piece(n):
            ns = slice(n * MXU_DIM, (n + 1) * MXU_DIM)
            acc_ref[:, ns] += _dot(pltpu.bitcast(a_ref[...], BF16), wdown_ref[c * FF_CHUNK:(c + 1) * FF_CHUNK, ns])

        return [functools.partial(piece, n) for n in range(D // MXU_DIM)]

    for s in range(n_chunks + 2):
        par = s % 2
        mxu, vpu = [], []
        if s < n_chunks:
            mxu += up_pieces(s, zs[par])
        if 1 <= s <= n_chunks:
            vpu += gate_pieces(s - 1, zs[1 - par], as_[1 - par])
        if s >= 2:
            mxu += down_pieces(s - 2, as_[par])
        for emit in _interleave(mxu, vpu):
            emit()
    o_ref[0] = x1_ref[0] + mod_ref[0, 5:6, :] * _rms(acc_ref[...], gpost_ref[...])


def _conv_ffn(h2, x1, mod, lw, *, latent, tm):
    B, S, D = x1.shape
    n_chunks = lw['w_down'].shape[1] // FF_CHUNK
    halo = BF16_SUBLANES
    nblk = tm // halo
    last = S // halo - 1
    mod_idx = (lambda b, i: (b + 1, 0, 0)) if latent else (lambda b, i: (0, 0, 0))
    tok = lambda b, i: (b, i, 0)
    z_scratch = pltpu.VMEM((FF_CHUNK // LANES, tm + 2 * halo, LANES), F32)
    a_scratch = pltpu.VMEM((tm // 2, FF_CHUNK), jnp.uint32)
    return pl.pallas_call(
        functools.partial(_conv_ffn_kernel, tm=tm, n_chunks=n_chunks, rb=FF_GATE_ROWS),
        out_shape=jax.ShapeDtypeStruct((B, S, D), F32),
        grid=(B, S // tm),
        in_specs=[pl.BlockSpec((1, tm, D), tok),
                  pl.BlockSpec((1, halo, D), lambda b, i: (b, jnp.maximum(i * nblk - 1, 0), 0)),
                  pl.BlockSpec((1, halo, D), lambda b, i: (b, jnp.minimum((i + 1) * nblk, last), 0)),
                  pl.BlockSpec((1, tm, D), tok),
                  pl.BlockSpec((1, F32_SUBLANES, D), mod_idx),
                  _layer_spec(lw['w_up'], lw['layer']),
                  _layer_spec(lw['conv'], lw['layer']),
                  _layer_spec(lw['w_down'], lw['layer']),
                  _layer_spec(lw['g_post_ffn'], lw['layer'])],
        out_specs=pl.BlockSpec((1, tm, D), tok),
        scratch_shapes=[pltpu.VMEM((tm + 2 * halo, D), BF16), z_scratch, z_scratch, z_scratch, z_scratch,
                        a_scratch, a_scratch, pltpu.VMEM((tm, D), F32)],
        compiler_params=_params(("parallel", "parallel")),
        name="conv_ffn",
    )(h2, h2, h2, x1, mod, lw['w_up'], lw['conv'], lw['w_down'], lw['g_post_ffn'])


def _rope_tables(S):
    n_rows = S // GRID_W
    rows = jnp.repeat(jnp.arange(n_rows, dtype=F32), GRID_W)
    cols = jnp.tile(jnp.arange(GRID_W, dtype=F32), n_rows)
    freqs = ROPE_BASE ** (-jnp.arange(ROPE_FREQS, dtype=F32) / ROPE_FREQS)
    ang = jnp.stack([rows[:, None] * freqs, cols[:, None] * freqs], axis=1)
    cos, sin = jnp.cos(ang), jnp.sin(ang)
    zero = jnp.zeros_like(sin)
    c = jnp.stack([cos, cos], axis=2).reshape(S, ROPE_DIM)
    s_up = jnp.stack([-sin, zero], axis=2).reshape(S, ROPE_DIM)
    s_dn = jnp.stack([zero, sin], axis=2).reshape(S, ROPE_DIM)

    def widen(t, lead, lead_value, scale):
        pads = [jnp.full((S, lead), lead_value, F32), t * scale, jnp.zeros((S, LANES - lead - ROPE_DIM), F32)]
        return jnp.concatenate(pads, axis=1)

    q_tabs = (widen(c, NOPE_DIM, Q_SCALE, Q_SCALE), widen(s_up + s_dn, NOPE_DIM, 0.0, Q_SCALE))
    k_tabs = (widen(c, 0, 0.0, 1.0), widen(s_up, 0, 0.0, 1.0), widen(s_dn, 0, 0.0, 1.0))
    return q_tabs + k_tabs


def _prepare_weights(g_pre_mix, g_post_mix, g_pre_ffn, g_post_ffn, w_in, g_q_a, w_q_b, g_kv_a, w_kv_b,
                     w_pool, pool_scale, g_sgu, w_sgu, b_sgu, w_out, w_up, conv_w, conv_b, w_down):
    depth, D, _ = w_in.shape
    o1 = Q_LORA
    o2 = o1 + KV_LORA
    o3 = o2 + ROPE_DIM
    w_in_r = jnp.concatenate([w_in[..., :o2], w_in[..., o3:], w_in[..., o2:o3],
                              jnp.zeros((depth, D, LANES - ROPE_DIM), F32)], axis=-1)

    qd = NOPE_DIM + ROPE_DIM
    wq = w_q_b.reshape(depth, Q_LORA, MLA_HEADS, qd)
    hz = jnp.zeros((depth, Q_LORA, MLA_HEADS, HEAD_PAD - qd), F32)
    wq_pad = jnp.concatenate([wq, hz], axis=-1).reshape(depth, Q_LORA, MLA_HEADS * HEAD_PAD)
    wq_rope = wq[..., NOPE_DIM:].reshape(depth, Q_LORA, MLA_HEADS, 2, 2, ROPE_FREQS)
    wq_partner = wq_rope[..., ::-1, :].reshape(depth, Q_LORA, MLA_HEADS, ROPE_DIM)
    wq_sw = jnp.concatenate([jnp.zeros((depth, Q_LORA, MLA_HEADS, NOPE_DIM), F32), wq_partner, hz],
                            axis=-1).reshape(depth, Q_LORA, MLA_HEADS * HEAD_PAD)

    wkv = w_kv_b.reshape(depth, KV_LORA, MLA_HEADS, NOPE_DIM + V_DIM)
    wk_t = jnp.transpose(wkv[..., :NOPE_DIM], (0, 2, 3, 1))
    wk_full = jnp.zeros((depth, MLA_HEADS, HEAD_PAD, KV_LORA + ROPE_DIM), F32)
    wk_full = wk_full.at[:, :, :NOPE_DIM, :KV_LORA].set(wk_t)
    wk_full = wk_full.at[:, :, NOPE_DIM:qd, KV_LORA:].set(jnp.eye(ROPE_DIM, dtype=F32))
    w_v = wkv[..., NOPE_DIM:].reshape(depth, KV_LORA, MLA_HEADS * V_DIM)

    wp = jnp.zeros((depth, POOL_W, POOL_W), F32)
    for g in range(len(POOL_WINDOWS)):
        wp = wp.at[:, g * POOL_GC:(g + 1) * POOL_GC, g * POOL_GC:(g + 1) * POOL_GC].set(w_pool[:, g])

    d_ff = w_down.shape[1]
    conv_pad = jnp.zeros((depth, F32_SUBLANES - conv_w.shape[1] - 1, 2 * d_ff), F32)
    conv = jnp.concatenate([conv_w, conv_b[:, None, :], conv_pad], axis=1)
    row = lambda a: a[:, None, :]
    return {
        'g_pre_mix': row(g_pre_mix), 'g_post_mix': row(g_post_mix),
        'g_pre_ffn': row(g_pre_ffn), 'g_post_ffn': row(g_post_ffn),
        'w_in': w_in_r.astype(BF16), 'g_q_a': row(g_q_a),
        'w_q_b': jnp.concatenate([wq_pad, wq_sw], axis=-1).astype(BF16),
        'g_kv_a': row(g_kv_a), 'wk_t': wk_full.reshape(depth, MLA_HEADS * HEAD_PAD, -1).astype(BF16),
        'w_v': w_v.astype(BF16), 'w_pool': wp.astype(BF16), 'pool_scale': row(pool_scale),
        'g_sgu': row(g_sgu), 'w_sgu': w_sgu.astype(BF16),
        'b_sgu': jnp.repeat(jnp.swapaxes(b_sgu, 1, 2), SGU_HD, axis=2),
        'w_out': w_out.astype(BF16), 'w_up': w_up.astype(BF16), 'w_down': w_down.astype(BF16), 'conv': conv,
    }


def _layer(x, mod, lw, tabs, ctx_ckr, *, latent, tm, tq):
    B, S, _ = x.shape
    fold = 1 if latent else max(1, min(B, TOKEN_TILE // S))
    folded = lambda a: a.reshape(B // fold, S * fold, a.shape[-1])
    unfolded = lambda a: a.reshape(B, S, a.shape[-1])
    q, ckr, pool_in, sgu, *cache = map(unfolded, _pre_mixer(folded(x), mod, lw, tabs, latent=latent, tm=tm * fold))
    ckr_parts = [ctx_ckr.astype(BF16), ckr] if latent else [ckr]
    attn = _attention(q, ckr_parts, lw, tq=tq, tc=min(ATTN_KEYS, *[p.shape[1] for p in ckr_parts]))
    pool = _pool(pool_in, lw)
    x1, h2 = map(unfolded, _post_mixer(*map(folded, (x, attn, pool, sgu)), mod, lw, latent=latent, tm=tm * fold))
    return _conv_ffn(h2, x1, mod, lw, latent=latent, tm=tm), cache


def kernel(x_prompt, x_sample, cache_ckv, cache_krope, c, c_ctx, w_mod, b_mod, g_pre_mix, g_post_mix, g_pre_ffn, g_post_ffn, w_in, g_q_a, w_q_b, g_kv_a, w_kv_b, w_pool, pool_scale, g_sgu, w_sgu, b_sgu, w_out, w_up, conv_w, conv_b, w_down):
    depth = w_in.shape[0]
    D = x_prompt.shape[-1]
    mod_rows = -(-(1 + c.shape[0]) // BF16_SUBLANES) * BF16_SUBLANES
    cc = jnp.concatenate([c_ctx[None, :], c, jnp.zeros((mod_rows - 1 - c.shape[0], D), F32)], axis=0)
    mod_all = _modulation(cc, w_mod, b_mod)
    n_vec = mod_all.shape[-1] // D
    mod_all = jnp.pad(mod_all.reshape(depth, mod_rows, n_vec, D), ((0, 0), (0, 0), (0, F32_SUBLANES - n_vec), (0, 0)))
    tabs = _rope_tables(x_sample.shape[1])

    xp, xs = x_prompt, x_sample
    ckv_list, kr_list = [], []
    weights = _prepare_weights(g_pre_mix, g_post_mix, g_pre_ffn, g_post_ffn, w_in, g_q_a, w_q_b, g_kv_a, w_kv_b,
                               w_pool, pool_scale, g_sgu, w_sgu, b_sgu, w_out, w_up, conv_w, conv_b, w_down)
    for l in range(depth):
        lw = dict(weights, layer=l)
        xp, (ckv, kr) = _layer(xp, mod_all[l], lw, None, None, latent=False, tm=xp.shape[1], tq=xp.shape[1])
        ckv_list.append(ckv)
        kr_list.append(kr)
        ctx_ckr = jnp.concatenate([cache_ckv[:, l], cache_krope[:, l]], axis=-1)
        xs, _ = _layer(xs, mod_all[l], lw, tabs, ctx_ckr, latent=True, tm=TOKEN_TILE, tq=ATTN_QUERIES)
    return (xp, xs, jnp.stack(ckv_list, axis=1), jnp.stack(kr_list, axis=1))
```

```python
import functools

import jax
import jax.numpy as jnp
from jax import lax
from jax.experimental import pallas as pl
from jax.experimental.pallas import tpu as pltpu

F32 = jnp.float32
BF16 = jnp.bfloat16

GRID_W = 64
MLA_HEADS = 8
NOPE_DIM = 64
ROPE_DIM = 32
ROPE_FREQS = ROPE_DIM // 4
V_DIM = 64
Q_LORA = 256
KV_LORA = 128
ROPE_BASE = 10000.0
SM_SCALE = (NOPE_DIM + ROPE_DIM) ** -0.5
LOG2E = 1.4426950408889634
Q_SCALE = SM_SCALE * LOG2E
POOL_WINDOWS = (2, 4, 8, 16)
POOL_GC = 64
POOL_W = POOL_GC * len(POOL_WINDOWS)
SGU_HEADS = 4
SGU_HD = 64
SGU_W = SGU_HEADS * SGU_HD
CHUNK = 128
EPS = 1e-6

LANES = 128
BF16_SUBLANES = 16
HEAD_PAD = LANES
VMEM_LIMIT = 62 * 1024 * 1024

C_QA = 0
C_KVA = C_QA + Q_LORA
C_POOL = C_KVA + KV_LORA
C_U = C_POOL + POOL_W
C_V = C_U + SGU_W
C_KR = C_V + SGU_W
IN_COLS_PAD = C_KR + LANES

FF_CHUNK = 256
FF_GATE_ROWS = 64
MXU_DIM = 256
ATTN_KEYS = 512
ATTN_QUERIES = 256
TOKEN_TILE = 512
POOL_PAD = 16
POOL_ROWS = 256
MOD_COLS = 1536
F32_SUBLANES = 8


def _rms(x, g):
    return x * lax.rsqrt(jnp.mean(x * x, axis=-1, keepdims=True) + EPS) * g


def _dot(a, b):
    return jnp.dot(a, b, preferred_element_type=F32)


def _params(sem):
    return pltpu.CompilerParams(dimension_semantics=sem, vmem_limit_bytes=VMEM_LIMIT)


def _layer_spec(w, l):
    nd = w.ndim - 1
    return pl.BlockSpec((None,) + w.shape[1:], lambda *_: (l,) + (0,) * nd)


def _mod_kernel(c_ref, w_ref, b_ref, o_ref):
    a = jax.nn.silu(c_ref[...]).astype(BF16)
    o_ref[0] = _dot(a, w_ref[0].astype(BF16)) + b_ref[0]


def _modulation(cc, w_mod, b_mod):
    depth, d, e = w_mod.shape
    rows = cc.shape[0]
    te = MOD_COLS
    return pl.pallas_call(
        _mod_kernel,
        out_shape=jax.ShapeDtypeStruct((depth, rows, e), F32),
        grid=(depth, e // te),
        in_specs=[
            pl.BlockSpec((rows, d), lambda l, j: (0, 0)),
            pl.BlockSpec((1, d, te), lambda l, j: (l, 0, j)),
            pl.BlockSpec((1, 1, te), lambda l, j: (l, 0, j)),
        ],
        out_specs=pl.BlockSpec((1, rows, te), lambda l, j: (l, 0, j)),
        compiler_params=_params(("parallel", "parallel")),
        name="modulation",
    )(cc, w_mod, b_mod.reshape(depth, 1, e))


def _rope(x, tabs):
    c_ref, up_ref, dn_ref = tabs
    up = pltpu.roll(x, LANES - ROPE_FREQS, axis=1)
    dn = pltpu.roll(x, ROPE_FREQS, axis=1)
    return x * c_ref[...] + up * up_ref[...] + dn * dn_ref[...]


def _pre_mixer_kernel(*refs, latent, tm):
    if latent:
        (x_ref, mod_ref, gpre_ref, win_ref, gqa_ref, wqb_ref, gkva_ref, gsgu_ref, wsgu_ref, bsgu_ref,
         *tabs, q_ref, ckr_ref, pool_ref, sgu_ref) = refs
    else:
        (x_ref, mod_ref, gpre_ref, win_ref, gqa_ref, wqb_ref, gkva_ref, gsgu_ref, wsgu_ref, bsgu_ref,
         q_ref, ckr_ref, pool_ref, sgu_ref, ckv_ref, kr_ref) = refs
    x = x_ref[0]
    shift = mod_ref[0, 0:1, :]
    scale = mod_ref[0, 1:2, :]
    h = _rms(x, gpre_ref[...]) * (1 + scale) + shift
    proj = _dot(h.astype(BF16), win_ref[...])

    qn = _rms(proj[:, C_QA:C_QA + Q_LORA], gqa_ref[...]).astype(BF16)
    if latent:
        qq = _dot(qn, wqb_ref[...])
    else:
        qq = _dot(qn, wqb_ref[:, 0:MLA_HEADS * HEAD_PAD])
    width = MLA_HEADS * HEAD_PAD
    for hh in range(MLA_HEADS):
        lo = hh * HEAD_PAD
        if latent:
            qh = qq[:, lo:lo + HEAD_PAD] * tabs[0][...] + qq[:, width + lo:width + lo + HEAD_PAD] * tabs[1][...]
        else:
            qh = qq[:, lo:lo + HEAD_PAD] * Q_SCALE
        q_ref[0, :, lo:lo + HEAD_PAD] = qh.astype(BF16)

    ckv = _rms(proj[:, C_KVA:C_KVA + KV_LORA], gkva_ref[...])
    kr = proj[:, C_KR:C_KR + LANES]
    if latent:
        kr = _rope(kr, tabs[2:])
    kr = kr[:, :ROPE_DIM]
    if not latent:
        ckv_ref[0] = ckv
        kr_ref[0] = kr
    ckr_ref[0, :, 0:KV_LORA] = ckv.astype(BF16)
    ckr_ref[0, :, KV_LORA:KV_LORA + ROPE_DIM] = kr.astype(BF16)
    pool_ref[0] = proj[:, C_POOL:C_POOL + POOL_W]

    u = proj[:, C_U:C_U + SGU_W]
    vn = _rms(proj[:, C_V:C_V + SGU_W], gsgu_ref[...]).astype(BF16)
    lane = lax.broadcasted_iota(jnp.int32, (1, LANES), 1)
    zero = jnp.zeros((), BF16)
    for c in range(tm // CHUNK):
        r0 = c * CHUNK
        for pair in range(SGU_W // LANES):
            l0 = pair * LANES
            vp = vn[r0:r0 + CHUNK, l0:l0 + LANES]
            v_lo = jnp.where(lane < SGU_HD, vp, zero)
            v_hi = jnp.where(lane >= SGU_HD, vp, zero)
            z = _dot(wsgu_ref[2 * pair], v_lo) + _dot(wsgu_ref[2 * pair + 1], v_hi) + bsgu_ref[:, l0:l0 + LANES]
            sgu_ref[0, r0:r0 + CHUNK, l0:l0 + LANES] = (u[r0:r0 + CHUNK, l0:l0 + LANES] * z).astype(BF16)


def _pre_mixer(x, mod, lw, tabs, *, latent, tm):
    B, S, D = x.shape
    mod_idx = (lambda i, b: (b + 1, 0, 0)) if latent else (lambda i, b: (0, 0, 0))
    tok = lambda i, b: (b, i, 0)
    names = ['g_pre_mix', 'w_in', 'g_q_a', 'w_q_b', 'g_kv_a', 'g_sgu', 'w_sgu', 'b_sgu']
    in_specs = [pl.BlockSpec((1, tm, D), tok), pl.BlockSpec((1, F32_SUBLANES, D), mod_idx)]
    in_specs += [_layer_spec(lw[n], lw['layer']) for n in names]
    args = [x, mod] + [lw[n] for n in names]
    if latent:
        in_specs += [pl.BlockSpec((tm, LANES), lambda i, b: (i, 0)) for _ in tabs]
        args += list(tabs)
    out_shape = [
        jax.ShapeDtypeStruct((B, S, MLA_HEADS * HEAD_PAD), BF16),
        jax.ShapeDtypeStruct((B, S, KV_LORA + ROPE_DIM), BF16),
        jax.ShapeDtypeStruct((B, S, POOL_W), F32),
        jax.ShapeDtypeStruct((B, S, SGU_W), BF16),
    ]
    out_specs = [
        pl.BlockSpec((1, tm, MLA_HEADS * HEAD_PAD), tok),
        pl.BlockSpec((1, tm, KV_LORA + ROPE_DIM), tok),
        pl.BlockSpec((1, tm, POOL_W), tok),
        pl.BlockSpec((1, tm, SGU_W), tok),
    ]
    if not latent:
        out_shape += [jax.ShapeDtypeStruct((B, S, KV_LORA), F32), jax.ShapeDtypeStruct((B, S, ROPE_DIM), F32)]
        out_specs += [pl.BlockSpec((1, tm, KV_LORA), tok), pl.BlockSpec((1, tm, ROPE_DIM), tok)]
    return pl.pallas_call(
        functools.partial(_pre_mixer_kernel, latent=latent, tm=tm),
        out_shape=out_shape,
        grid=(S // tm, B),
        in_specs=in_specs,
        out_specs=out_specs,
        compiler_params=_params(("parallel", "parallel")),
        name="pre_mixer_lat" if latent else "pre_mixer_ctx",
    )(*args)


def _pool_kernel(x_ref, w_ref, scale_ref, o_ref, pad_ref, *, S):
    nslab = POOL_W // LANES
    zeros = jnp.zeros((POOL_PAD, LANES), F32)
    for slab in range(nslab):
        pad_ref[slab, 0:POOL_PAD, :] = zeros
        pad_ref[slab, POOL_PAD + S:POOL_PAD + S + POOL_PAD, :] = zeros
        pad_ref[slab, POOL_PAD:POOL_PAD + S, :] = x_ref[0, :, slab * LANES:(slab + 1) * LANES]

    R = min(POOL_ROWS, S)
    n = R // 2
    lane = lax.broadcasted_iota(jnp.int32, (1, LANES), 1)
    lane_w = lax.broadcasted_iota(jnp.int32, (1, POOL_W), 1)
    half = jnp.where(lane_w < POOL_GC, POOL_WINDOWS[0] // 2,
                     jnp.where(lane_w < 2 * POOL_GC, POOL_WINDOWS[1] // 2,
                               jnp.where(lane_w < 3 * POOL_GC, POOL_WINDOWS[2] // 2, POOL_WINDOWS[3] // 2)))
    for r in range(S // R):
        base = POOL_PAD + r * R

        def window_sums(slab, lo_small, lo_big, par):
            def ld(off):
                return pad_ref[slab, pl.ds(base + par + off, n, stride=2), :]

            x = ld(0)
            small = x
            for off in range(-lo_small, lo_small):
                if off != 0:
                    small = small + ld(off)
            big = small
            for off in list(range(-lo_big, -lo_small)) + list(range(lo_small, lo_big)):
                big = big + ld(off)
            return x, jnp.where(lane < POOL_GC, small, big)

        ds = []
        for par in range(2):
            xa, tot_a = window_sums(0, POOL_WINDOWS[0] // 2, POOL_WINDOWS[1] // 2, par)
            xb, tot_b = window_sums(1, POOL_WINDOWS[2] // 2, POOL_WINDOWS[3] // 2, par)
            t = r * R + par + 2 * lax.broadcasted_iota(jnp.int32, (n, 1), 0)
            cnt = (jnp.minimum(t + half, S) - jnp.maximum(t - half, 0)).astype(F32)
            ds.append(jnp.concatenate([tot_a, tot_b], axis=1) / cnt - jnp.concatenate([xa, xb], axis=1))
        d = pltpu.bitcast(pltpu.pack_elementwise(ds, packed_dtype=BF16), BF16)
        y = _dot(d, w_ref[...]) * scale_ref[...]
        o_ref[0, r * R:(r + 1) * R, :] = y.astype(BF16)


def _pool(pool_in, lw):
    B, S, W = pool_in.shape
    return pl.pallas_call(
        functools.partial(_pool_kernel, S=S),
        out_shape=jax.ShapeDtypeStruct((B, S, W), BF16),
        grid=(B,),
        in_specs=[pl.BlockSpec((1, S, W), lambda b: (b, 0, 0)), _layer_spec(lw['w_pool'], lw['layer']),
                  _layer_spec(lw['pool_scale'], lw['layer'])],
        out_specs=pl.BlockSpec((1, S, W), lambda b: (b, 0, 0)),
        scratch_shapes=[pltpu.VMEM((W // LANES, S + 2 * POOL_PAD, LANES), F32)],
        compiler_params=_params(("parallel",)),
        name="pool",
    )(pool_in, lw['w_pool'], lw['pool_scale'])


def _attention_kernel(q_ref, *refs, tq, T, tc):
    *ckr_refs, wk_ref, wv_ref, vones_ref, o_ref, kt_ref, v_ref, s_ref = refs
    nc = T // tc
    lane = lax.broadcasted_iota(jnp.int32, (1, LANES), 1)
    chunk_src = [(r, k) for r in ckr_refs for k in range(r.shape[1] // tc)]

    @pl.when(pl.program_id(1) == 0)
    def _():
        for c in range(nc):
            src, k = chunk_src[c]
            a = src[0, k * tc:(k + 1) * tc, :]
            kt = lax.dot_general(wk_ref[...], a, (((1,), (1,)), ((), ())), preferred_element_type=F32)
            kt_ref[:, c * tc:(c + 1) * tc] = kt.astype(BF16)
            v_ref[c * tc:(c + 1) * tc, :] = (_dot(a[:, :KV_LORA], wv_ref[...]) + vones_ref[...]).astype(BF16)

    def scores(h, c, mrun):
        s = _dot(q_ref[0, :, h * HEAD_PAD:(h + 1) * HEAD_PAD],
                 kt_ref[h * HEAD_PAD:(h + 1) * HEAD_PAD, c * tc:(c + 1) * tc])
        s_ref[h % 2, :, c * tc:(c + 1) * tc] = s
        for g in range(tc // LANES):
            mrun = jnp.maximum(mrun, s[:, g * LANES:(g + 1) * LANES])
        return mrun

    def probs_pv(h, c, m, acc):
        p = jnp.exp2(s_ref[h % 2, :, c * tc:(c + 1) * tc] - m)
        return acc + _dot(p.astype(BF16), v_ref[c * tc:(c + 1) * tc, h * LANES:(h + 1) * LANES])

    m_prev = None
    pair_out = None
    for slot in range(MLA_HEADS + 1):
        mrun = jnp.full((tq, LANES), -jnp.inf, F32)
        acc = jnp.zeros((tq, LANES), F32)
        for c in range(nc):
            if slot < MLA_HEADS:
                mrun = scores(slot, c, mrun)
            if slot >= 1:
                acc = probs_pv(slot - 1, c, m_prev, acc)
        if slot >= 1:
            h = slot - 1
            o = acc / pltpu.roll(acc, V_DIM, axis=1)
            if h % 2 == 0:
                pair_out = o
            else:
                hp = h // 2
                o_ref[0, :, hp * LANES:(hp + 1) * LANES] = jnp.where(lane < V_DIM, pair_out, o).astype(BF16)
        if slot < MLA_HEADS:
            m_prev = jnp.max(mrun, axis=-1, keepdims=True)


def _attention(q, ckr_parts, lw, *, tq, tc):
    B, S, _ = q.shape
    W = ckr_parts[0].shape[-1]
    T = sum(p.shape[1] for p in ckr_parts)
    assert all(p.shape[1] % tc == 0 for p in ckr_parts)
    return pl.pallas_call(
        functools.partial(_attention_kernel, tq=tq, T=T, tc=tc),
        out_shape=jax.ShapeDtypeStruct((B, S, MLA_HEADS * V_DIM), BF16),
        grid=(B, S // tq),
        in_specs=[pl.BlockSpec((1, tq, MLA_HEADS * HEAD_PAD), lambda b, i: (b, i, 0))]
                 + [pl.BlockSpec((1, p.shape[1], W), lambda b, i: (b, 0, 0)) for p in ckr_parts]
                 + [_layer_spec(lw[n], lw['layer']) for n in ('wk_t', 'w_v', 'v_ones')],
        out_specs=pl.BlockSpec((1, tq, MLA_HEADS * V_DIM), lambda b, i: (b, i, 0)),
        scratch_shapes=[pltpu.VMEM((MLA_HEADS * HEAD_PAD, T), BF16), pltpu.VMEM((T, MLA_HEADS * LANES), BF16),
                        pltpu.VMEM((2, tq, T), F32)],
        compiler_params=_params(("arbitrary", "arbitrary")),
        name="attention",
    )(q, *ckr_parts, lw['wk_t'], lw['w_v'], lw['v_ones'])


def _post_mixer_kernel(x_ref, attn_ref, pool_ref, sgu_ref, mod_ref, wout_ref, gpost_ref, gffn_ref, x1_ref, h2_ref,
                       *, tm):
    D = x_ref.shape[-1]
    half = tm // 2
    rb = half // (D // MXU_DIM)

    def mix(r0):
        rows = slice(r0, r0 + half)
        return jnp.concatenate([attn_ref[0, rows, :], pool_ref[0, rows, :], sgu_ref[0, rows, :]], axis=-1)

    def finish(out, r0):
        x1 = x_ref[0, r0:r0 + out.shape[0], :] + mod_ref[0, 2:3, :] * _rms(out, gpost_ref[...])
        x1_ref[0, r0:r0 + out.shape[0], :] = x1
        h2 = _rms(x1, gffn_ref[...]) * (1 + mod_ref[0, 4:5, :]) + mod_ref[0, 3:4, :]
        h2_ref[0, r0:r0 + out.shape[0], :] = h2.astype(BF16)

    out_a = _dot(mix(0), wout_ref[...])
    mix_b = mix(half)
    cols = []
    for j in range(D // MXU_DIM):
        cols.append(_dot(mix_b, wout_ref[:, j * MXU_DIM:(j + 1) * MXU_DIM]))
        finish(out_a[j * rb:(j + 1) * rb, :], j * rb)
    finish(jnp.concatenate(cols, axis=-1), half)


def _post_mixer(x, attn, pool, sgu, mod, lw, *, latent, tm):
    B, S, D = x.shape
    mod_idx = (lambda b, i: (b + 1, 0, 0)) if latent else (lambda b, i: (0, 0, 0))
    tok = lambda b, i: (b, i, 0)
    return pl.pallas_call(
        functools.partial(_post_mixer_kernel, tm=tm),
        out_shape=[jax.ShapeDtypeStruct((B, S, D), F32), jax.ShapeDtypeStruct((B, S, D), BF16)],
        grid=(B, S // tm),
        in_specs=[pl.BlockSpec((1, tm, D), tok),
                  pl.BlockSpec((1, tm, attn.shape[-1]), tok),
                  pl.BlockSpec((1, tm, POOL_W), tok),
                  pl.BlockSpec((1, tm, SGU_W), tok),
                  pl.BlockSpec((1, F32_SUBLANES, D), mod_idx),
                  _layer_spec(lw['w_out'], lw['layer']),
                  _layer_spec(lw['g_post_mix'], lw['layer']),
                  _layer_spec(lw['g_pre_ffn'], lw['layer'])],
        out_specs=[pl.BlockSpec((1, tm, D), tok), pl.BlockSpec((1, tm, D), tok)],
        compiler_params=_params(("parallel", "parallel")),
        name="post_mixer",
    )(x, attn, pool, sgu, mod, lw['w_out'], lw['g_post_mix'], lw['g_pre_ffn'])


def _interleave(a, b):
    out, ia, ib = [], 0, 0
    while ia < len(a) or ib < len(b):
        if ib >= len(b) or (ia < len(a) and ia * len(b) <= ib * len(a)):
            out.append(a[ia])
            ia += 1
        else:
            out.append(b[ib])
            ib += 1
    return out


def _conv_ffn_kernel(h_ref, hprev_ref, hnext_ref, x1_ref, mod_ref, wup_ref, cw_ref, wdown_ref, gpost_ref, o_ref,
                     hext_ref, zg0_ref, zv0_ref, zg1_ref, zv1_ref, a0_ref, a1_ref, acc_ref, *, tm, n_chunks, rb):
    i = pl.program_id(1)
    halo = BF16_SUBLANES
    D = hext_ref.shape[1]
    zero = jnp.zeros((), BF16)
    hext_ref[0:halo, :] = jnp.where(i > 0, hprev_ref[0], zero)
    hext_ref[halo:halo + tm, :] = h_ref[0]
    hext_ref[halo + tm:halo + tm + halo, :] = jnp.where(i < pl.num_programs(1) - 1, hnext_ref[0], zero)
    acc_ref[...] = jnp.zeros_like(acc_ref)
    zs = ((zg0_ref, zv0_ref), (zg1_ref, zv1_ref))
    as_ = (a0_ref, a1_ref)

    def conv_pair(z_ref, slab, cw, r0):
        n = rb // 2
        base = halo + r0
        e_m = z_ref[slab, pl.ds(base - 1, n, stride=2), :]
        e_c = z_ref[slab, pl.ds(base, n, stride=2), :]
        e_p = z_ref[slab, pl.ds(base + 1, n, stride=2), :]
        e_pp = z_ref[slab, pl.ds(base + 2, n, stride=2), :]
        w0, w1, w2, b = cw[0:1, :], cw[1:2, :], cw[2:3, :], cw[3:4, :]
        even = e_m * w0 + e_c * w1 + e_p * w2 + b
        odd = e_c * w0 + e_p * w1 + e_pp * w2 + b
        return even, odd

    def up_pieces(c, z_pair):
        pieces = []
        for which in range(2):
            state = {}

            def piece(k, which=which, state=state):
                col = (which * n_chunks + c) * FF_CHUNK
                part = _dot(hext_ref[:, k * MXU_DIM:(k + 1) * MXU_DIM],
                            wup_ref[k * MXU_DIM:(k + 1) * MXU_DIM, col:col + FF_CHUNK])
                state['z'] = part if k == 0 else state['z'] + part
                if k == D // MXU_DIM - 1:
                    for slab in range(FF_CHUNK // LANES):
                        z_pair[which][slab] = state['z'][:, slab * LANES:(slab + 1) * LANES]

            pieces += [functools.partial(piece, k) for k in range(D // MXU_DIM)]
        return pieces

    def gate_pieces(c, z_pair, a_ref):
        def piece(r0):
            for slab in range(FF_CHUNK // LANES):
                ls = slice(slab * LANES, (slab + 1) * LANES)
                col = c * FF_CHUNK + slab * LANES
                cwg = cw_ref[:, col:col + LANES]
                cwv = cw_ref[:, n_chunks * FF_CHUNK + col:n_chunks * FF_CHUNK + col + LANES]
                g_e, g_o = conv_pair(z_pair[0], slab, cwg, r0)
                v_e, v_o = conv_pair(z_pair[1], slab, cwv, r0)
                a_ref[r0 // 2:(r0 + rb) // 2, ls] = pltpu.pack_elementwise(
                    [jax.nn.silu(g_e) * v_e, jax.nn.silu(g_o) * v_o], packed_dtype=BF16)

        return [functools.partial(piece, r * rb) for r in range(tm // rb)]

    def down_pieces(c, a_ref):
        def piece(n):
            ns = slice(n * MXU_DIM, (n + 1) * MXU_DIM)
            acc_ref[:, ns] += _dot(pltpu.bitcast(a_ref[...], BF16), wdown_ref[c * FF_CHUNK:(c + 1) * FF_CHUNK, ns])

        return [functools.partial(piece, n) for n in range(D // MXU_DIM)]

    for s in range(n_chunks + 2):
        par = s % 2
        mxu, vpu = [], []
        if s < n_chunks:
            mxu += up_pieces(s, zs[par])
        if 1 <= s <= n_chunks:
            vpu += gate_pieces(s - 1, zs[1 - par], as_[1 - par])
        if s >= 2:
            mxu += down_pieces(s - 2, as_[par])
        for emit in _interleave(mxu, vpu):
            emit()
    o_ref[0] = x1_ref[0] + mod_ref[0, 5:6, :] * _rms(acc_ref[...], gpost_ref[...])


def _conv_ffn(h2, x1, mod, lw, *, latent, tm):
    B, S, D = x1.shape
    n_chunks = lw['w_down'].shape[1] // FF_CHUNK
    halo = BF16_SUBLANES
    nblk = tm // halo
    last = S // halo - 1
    mod_idx = (lambda b, i: (b + 1, 0, 0)) if latent else (lambda b, i: (0, 0, 0))
    tok = lambda b, i: (b, i, 0)
    z_scratch = pltpu.VMEM((FF_CHUNK // LANES, tm + 2 * halo, LANES), F32)
    a_scratch = pltpu.VMEM((tm // 2, FF_CHUNK), jnp.uint32)
    return pl.pallas_call(
        functools.partial(_conv_ffn_kernel, tm=tm, n_chunks=n_chunks, rb=FF_GATE_ROWS),
        out_shape=jax.ShapeDtypeStruct((B, S, D), F32),
        grid=(B, S // tm),
        in_specs=[pl.BlockSpec((1, tm, D), tok),
                  pl.BlockSpec((1, halo, D), lambda b, i: (b, jnp.maximum(i * nblk - 1, 0), 0)),
                  pl.BlockSpec((1, halo, D), lambda b, i: (b, jnp.minimum((i + 1) * nblk, last), 0)),
                  pl.BlockSpec((1, tm, D), tok),
                  pl.BlockSpec((1, F32_SUBLANES, D), mod_idx),
                  _layer_spec(lw['w_up'], lw['layer']),
                  _layer_spec(lw['conv'], lw['layer']),
                  _layer_spec(lw['w_down'], lw['layer']),
                  _layer_spec(lw['g_post_ffn'], lw['layer'])],
        out_specs=pl.BlockSpec((1, tm, D), tok),
        scratch_shapes=[pltpu.VMEM((tm + 2 * halo, D), BF16), z_scratch, z_scratch, z_scratch, z_scratch,
                        a_scratch, a_scratch, pltpu.VMEM((tm, D), F32)],
        compiler_params=_params(("parallel", "parallel")),
        name="conv_ffn",
    )(h2, h2, h2, x1, mod, lw['w_up'], lw['conv'], lw['w_down'], lw['g_post_ffn'])


def _rope_tables(S):
    n_rows = S // GRID_W
    rows = jnp.repeat(jnp.arange(n_rows, dtype=F32), GRID_W)
    cols = jnp.tile(jnp.arange(GRID_W, dtype=F32), n_rows)
    freqs = ROPE_BASE ** (-jnp.arange(ROPE_FREQS, dtype=F32) / ROPE_FREQS)
    ang = jnp.stack([rows[:, None] * freqs, cols[:, None] * freqs], axis=1)
    cos, sin = jnp.cos(ang), jnp.sin(ang)
    zero = jnp.zeros_like(sin)
    c = jnp.stack([cos, cos], axis=2).reshape(S, ROPE_DIM)
    s_up = jnp.stack([-sin, zero], axis=2).reshape(S, ROPE_DIM)
    s_dn = jnp.stack([zero, sin], axis=2).reshape(S, ROPE_DIM)

    def widen(t, lead, lead_value, scale):
        pads = [jnp.full((S, lead), lead_value, F32), t * scale, jnp.zeros((S, LANES - lead - ROPE_DIM), F32)]
        return jnp.concatenate(pads, axis=1)

    q_tabs = (widen(c, NOPE_DIM, Q_SCALE, Q_SCALE), widen(s_up + s_dn, NOPE_DIM, 0.0, Q_SCALE))
    k_tabs = (widen(c, 0, 0.0, 1.0), widen(s_up, 0, 0.0, 1.0), widen(s_dn, 0, 0.0, 1.0))
    return q_tabs + k_tabs


def _prepare_weights(g_pre_mix, g_post_mix, g_pre_ffn, g_post_ffn, w_in, g_q_a, w_q_b, g_kv_a, w_kv_b,
                     w_pool, pool_scale, g_sgu, w_sgu, b_sgu, w_out, w_up, conv_w, conv_b, w_down):
    depth, D, _ = w_in.shape
    o1 = Q_LORA
    o2 = o1 + KV_LORA
    o3 = o2 + ROPE_DIM
    w_in_r = jnp.concatenate([w_in[..., :o2], w_in[..., o3:], w_in[..., o2:o3],
                              jnp.zeros((depth, D, LANES - ROPE_DIM), F32)], axis=-1)

    qd = NOPE_DIM + ROPE_DIM
    wq = w_q_b.reshape(depth, Q_LORA, MLA_HEADS, qd)
    hz = jnp.zeros((depth, Q_LORA, MLA_HEADS, HEAD_PAD - qd), F32)
    wq_pad = jnp.concatenate([wq, hz], axis=-1).reshape(depth, Q_LORA, MLA_HEADS * HEAD_PAD)
    wq_rope = wq[..., NOPE_DIM:].reshape(depth, Q_LORA, MLA_HEADS, 2, 2, ROPE_FREQS)
    wq_partner = wq_rope[..., ::-1, :].reshape(depth, Q_LORA, MLA_HEADS, ROPE_DIM)
    wq_sw = jnp.concatenate([jnp.zeros((depth, Q_LORA, MLA_HEADS, NOPE_DIM), F32), wq_partner, hz],
                            axis=-1).reshape(depth, Q_LORA, MLA_HEADS * HEAD_PAD)

    wkv = w_kv_b.reshape(depth, KV_LORA, MLA_HEADS, NOPE_DIM + V_DIM)
    wk_t = jnp.transpose(wkv[..., :NOPE_DIM], (0, 2, 3, 1))
    wk_full = jnp.zeros((depth, MLA_HEADS, HEAD_PAD, KV_LORA + ROPE_DIM), F32)
    wk_full = wk_full.at[:, :, :NOPE_DIM, :KV_LORA].set(wk_t)
    wk_full = wk_full.at[:, :, NOPE_DIM:qd, KV_LORA:].set(jnp.eye(ROPE_DIM, dtype=F32))
    is_v = (jnp.arange(MLA_HEADS)[:, None] % 2 == jnp.arange(2)[None, :])[..., None]
    w_v = jnp.where(is_v[None, None], wkv[..., None, NOPE_DIM:], 0.0).reshape(depth, KV_LORA, MLA_HEADS * LANES)
    v_ones = jnp.broadcast_to(jnp.where(is_v, 0.0, 1.0), (depth, MLA_HEADS, 2, V_DIM)).reshape(depth, 1, -1)

    wp = jnp.zeros((depth, POOL_W, POOL_W), F32)
    for g in range(len(POOL_WINDOWS)):
        wp = wp.at[:, g * POOL_GC:(g + 1) * POOL_GC, g * POOL_GC:(g + 1) * POOL_GC].set(w_pool[:, g])

    d_ff = w_down.shape[1]
    conv_pad = jnp.zeros((depth, F32_SUBLANES - conv_w.shape[1] - 1, 2 * d_ff), F32)
    conv = jnp.concatenate([conv_w, conv_b[:, None, :], conv_pad], axis=1)
    row = lambda a: a[:, None, :]
    return {
        'g_pre_mix': row(g_pre_mix), 'g_post_mix': row(g_post_mix),
        'g_pre_ffn': row(g_pre_ffn), 'g_post_ffn': row(g_post_ffn),
        'w_in': w_in_r.astype(BF16), 'g_q_a': row(g_q_a),
        'w_q_b': jnp.concatenate([wq_pad, wq_sw], axis=-1).astype(BF16),
        'g_kv_a': row(g_kv_a), 'wk_t': wk_full.reshape(depth, MLA_HEADS * HEAD_PAD, -1).astype(BF16),
        'w_v': w_v.astype(BF16), 'v_ones': v_ones.astype(F32), 'w_pool': wp.astype(BF16), 'pool_scale': row(pool_scale),
        'g_sgu': row(g_sgu), 'w_sgu': w_sgu.astype(BF16),
        'b_sgu': jnp.repeat(jnp.swapaxes(b_sgu, 1, 2), SGU_HD, axis=2),
        'w_out': w_out.astype(BF16), 'w_up': w_up.astype(BF16), 'w_down': w_down.astype(BF16), 'conv': conv,
    }


def _layer(x, mod, lw, tabs, ctx_ckr, *, latent, tm, tq):
    B, S, _ = x.shape
    fold = 1 if latent else max(1, min(B, TOKEN_TILE // S))
    folded = lambda a: a.reshape(B // fold, S * fold, a.shape[-1])
    unfolded = lambda a: a.reshape(B, S, a.shape[-1])
    q, ckr, pool_in, sgu, *cache = map(unfolded, _pre_mixer(folded(x), mod, lw, tabs, latent=latent, tm=tm * fold))
    ckr_parts = [ctx_ckr.astype(BF16), ckr] if latent else [ckr]
    attn = _attention(q, ckr_parts, lw, tq=tq, tc=min(ATTN_KEYS, *[p.shape[1] for p in ckr_parts]))
    pool = _pool(pool_in, lw)
    x1, h2 = map(unfolded, _post_mixer(*map(folded, (x, attn, pool, sgu)), mod, lw, latent=latent, tm=tm * fold))
    return _conv_ffn(h2, x1, mod, lw, latent=latent, tm=tm), cache


def kernel(x_prompt, x_sample, cache_ckv, cache_krope, c, c_ctx, w_mod, b_mod, g_pre_mix, g_post_mix, g_pre_ffn, g_post_ffn, w_in, g_q_a, w_q_b, g_kv_a, w_kv_b, w_pool, pool_scale, g_sgu, w_sgu, b_sgu, w_out, w_up, conv_w, conv_b, w_down):
    depth = w_in.shape[0]
    D = x_prompt.shape[-1]
    mod_rows = -(-(1 + c.shape[0]) // BF16_SUBLANES) * BF16_SUBLANES
    cc = jnp.concatenate([c_ctx[None, :], c, jnp.zeros((mod_rows - 1 - c.shape[0], D), F32)], axis=0)
    mod_all = _modulation(cc, w_mod, b_mod)
    n_vec = mod_all.shape[-1] // D
    mod_all = jnp.pad(mod_all.reshape(depth, mod_rows, n_vec, D), ((0, 0), (0, 0), (0, F32_SUBLANES - n_vec), (0, 0)))
    tabs = _rope_tables(x_sample.shape[1])

    xp, xs = x_prompt, x_sample
    ckv_list, kr_list = [], []
    weights = _prepare_weights(g_pre_mix, g_post_mix, g_pre_ffn, g_post_ffn, w_in, g_q_a, w_q_b, g_kv_a, w_kv_b,
                               w_pool, pool_scale, g_sgu, w_sgu, b_sgu, w_out, w_up, conv_w, conv_b, w_down)
    for l in range(depth):
        lw = dict(weights, layer=l)
        xp, (ckv, kr) = _layer(xp, mod_all[l], lw, None, None, latent=False, tm=xp.shape[1], tq=xp.shape[1])
        ckv_list.append(ckv)
        kr_list.append(kr)
        ctx_ckr = jnp.concatenate([cache_ckv[:, l], cache_krope[:, l]], axis=-1)
        xs, _ = _layer(xs, mod_all[l], lw, tabs, ctx_ckr, latent=True, tm=TOKEN_TILE, tq=ATTN_QUERIES)
    return (xp, xs, jnp.stack(ckv_list, axis=1), jnp.stack(kr_list, axis=1))
```

```python
import functools

import jax
import jax.numpy as jnp
from jax import lax
from jax.experimental import pallas as pl
from jax.experimental.pallas import tpu as pltpu

F32 = jnp.float32
BF16 = jnp.bfloat16

GRID_W = 64
MLA_HEADS = 8
NOPE_DIM = 64
ROPE_DIM = 32
ROPE_FREQS = ROPE_DIM // 4
V_DIM = 64
Q_LORA = 256
KV_LORA = 128
ROPE_BASE = 10000.0
SM_SCALE = (NOPE_DIM + ROPE_DIM) ** -0.5
LOG2E = 1.4426950408889634
Q_SCALE = SM_SCALE * LOG2E
POOL_WINDOWS = (2, 4, 8, 16)
POOL_GC = 64
POOL_W = POOL_GC * len(POOL_WINDOWS)
SGU_HEADS = 4
SGU_HD = 64
SGU_W = SGU_HEADS * SGU_HD
CHUNK = 128
EPS = 1e-6

LANES = 128
BF16_SUBLANES = 16
HEAD_PAD = LANES
VMEM_LIMIT = 62 * 1024 * 1024

C_QA = 0
C_KVA = C_QA + Q_LORA
C_POOL = C_KVA + KV_LORA
C_U = C_POOL + POOL_W
C_V = C_U + SGU_W
C_KR = C_V + SGU_W
IN_COLS_PAD = C_KR + LANES

FF_CHUNK = 256
FF_GATE_ROWS = 64
MXU_DIM = 256
ATTN_KEYS = 512
ATTN_QUERIES = 256
TOKEN_TILE = 512
POOL_PAD = 16
POOL_ROWS = 256
MOD_COLS = 1536
F32_SUBLANES = 8


def _rms(x, g):
    return x * lax.rsqrt(jnp.mean(x * x, axis=-1, keepdims=True) + EPS) * g


def _dot(a, b):
    return jnp.dot(a, b, preferred_element_type=F32)


def _params(sem):
    return pltpu.CompilerParams(dimension_semantics=sem, vmem_limit_bytes=VMEM_LIMIT)


def _layer_spec(w, l):
    nd = w.ndim - 1
    return pl.BlockSpec((None,) + w.shape[1:], lambda *_: (l,) + (0,) * nd)


def _mod_kernel(c_ref, w_ref, b_ref, o_ref):
    a = jax.nn.silu(c_ref[...]).astype(BF16)
    o_ref[0] = _dot(a, w_ref[0].astype(BF16)) + b_ref[0]


def _modulation(cc, w_mod, b_mod):
    depth, d, e = w_mod.shape
    rows = cc.shape[0]
    te = MOD_COLS
    return pl.pallas_call(
        _mod_kernel,
        out_shape=jax.ShapeDtypeStruct((depth, rows, e), F32),
        grid=(depth, e // te),
        in_specs=[
            pl.BlockSpec((rows, d), lambda l, j: (0, 0)),
            pl.BlockSpec((1, d, te), lambda l, j: (l, 0, j)),
            pl.BlockSpec((1, 1, te), lambda l, j: (l, 0, j)),
        ],
        out_specs=pl.BlockSpec((1, rows, te), lambda l, j: (l, 0, j)),
        compiler_params=_params(("parallel", "parallel")),
        name="modulation",
    )(cc, w_mod, b_mod.reshape(depth, 1, e))


def _rope(x, tabs):
    c_ref, up_ref, dn_ref = tabs
    up = pltpu.roll(x, LANES - ROPE_FREQS, axis=1)
    dn = pltpu.roll(x, ROPE_FREQS, axis=1)
    return x * c_ref[...] + up * up_ref[...] + dn * dn_ref[...]


def _pre_mixer_kernel(*refs, latent, tm):
    if latent:
        (x_ref, mod_ref, gpre_ref, win_ref, gqa_ref, wqb_ref, gkva_ref, gsgu_ref, wsgu_ref, bsgu_ref,
         *tabs, q_ref, ckr_ref, pool_ref, sgu_ref) = refs
    else:
        (x_ref, mod_ref, gpre_ref, win_ref, gqa_ref, wqb_ref, gkva_ref, gsgu_ref, wsgu_ref, bsgu_ref,
         q_ref, ckr_ref, pool_ref, sgu_ref, ckv_ref, kr_ref) = refs
    x = x_ref[0]
    shift = mod_ref[0, 0:1, :]
    scale = mod_ref[0, 1:2, :]
    h = _rms(x, gpre_ref[...]) * (1 + scale) + shift
    proj = _dot(h.astype(BF16), win_ref[...])

    qn = _rms(proj[:, C_QA:C_QA + Q_LORA], gqa_ref[...]).astype(BF16)
    if latent:
        qq = _dot(qn, wqb_ref[...])
    else:
        qq = _dot(qn, wqb_ref[:, 0:MLA_HEADS * HEAD_PAD])
    width = MLA_HEADS * HEAD_PAD
    for hh in range(MLA_HEADS):
        lo = hh * HEAD_PAD
        if latent:
            qh = qq[:, lo:lo + HEAD_PAD] * tabs[0][...] + qq[:, width + lo:width + lo + HEAD_PAD] * tabs[1][...]
        else:
            qh = qq[:, lo:lo + HEAD_PAD] * Q_SCALE
        q_ref[0, :, lo:lo + HEAD_PAD] = qh.astype(BF16)

    ckv = _rms(proj[:, C_KVA:C_KVA + KV_LORA], gkva_ref[...])
    kr = proj[:, C_KR:C_KR + LANES]
    if latent:
        kr = _rope(kr, tabs[2:])
    kr = kr[:, :ROPE_DIM]
    if not latent:
        ckv_ref[0] = ckv
        kr_ref[0] = kr
    ckr_ref[0, :, 0:KV_LORA] = ckv.astype(BF16)
    ckr_ref[0, :, KV_LORA:KV_LORA + ROPE_DIM] = kr.astype(BF16)
    pool_ref[0] = proj[:, C_POOL:C_POOL + POOL_W]

    u = proj[:, C_U:C_U + SGU_W]
    vn = _rms(proj[:, C_V:C_V + SGU_W], gsgu_ref[...]).astype(BF16)
    lane = lax.broadcasted_iota(jnp.int32, (1, LANES), 1)
    zero = jnp.zeros((), BF16)
    for c in range(tm // CHUNK):
        r0 = c * CHUNK
        for pair in range(SGU_W // LANES):
            l0 = pair * LANES
            vp = vn[r0:r0 + CHUNK, l0:l0 + LANES]
            v_lo = jnp.where(lane < SGU_HD, vp, zero)
            v_hi = jnp.where(lane >= SGU_HD, vp, zero)
            z = _dot(wsgu_ref[2 * pair], v_lo) + _dot(wsgu_ref[2 * pair + 1], v_hi) + bsgu_ref[:, l0:l0 + LANES]
            sgu_ref[0, r0:r0 + CHUNK, l0:l0 + LANES] = (u[r0:r0 + CHUNK, l0:l0 + LANES] * z).astype(BF16)


def _pre_mixer(x, mod, lw, tabs, *, latent, tm):
    B, S, D = x.shape
    mod_idx = (lambda i, b: (b + 1, 0, 0)) if latent else (lambda i, b: (0, 0, 0))
    tok = lambda i, b: (b, i, 0)
    names = ['g_pre_mix', 'w_in', 'g_q_a', 'w_q_b', 'g_kv_a', 'g_sgu', 'w_sgu', 'b_sgu']
    in_specs = [pl.BlockSpec((1, tm, D), tok), pl.BlockSpec((1, F32_SUBLANES, D), mod_idx)]
    in_specs += [_layer_spec(lw[n], lw['layer']) for n in names]
    args = [x, mod] + [lw[n] for n in names]
    if latent:
        in_specs += [pl.BlockSpec((tm, LANES), lambda i, b: (i, 0)) for _ in tabs]
        args += list(tabs)
    out_shape = [
        jax.ShapeDtypeStruct((B, S, MLA_HEADS * HEAD_PAD), BF16),
        jax.ShapeDtypeStruct((B, S, KV_LORA + ROPE_DIM), BF16),
        jax.ShapeDtypeStruct((B, S, POOL_W), F32),
        jax.ShapeDtypeStruct((B, S, SGU_W), BF16),
    ]
    out_specs = [
        pl.BlockSpec((1, tm, MLA_HEADS * HEAD_PAD), tok),
        pl.BlockSpec((1, tm, KV_LORA + ROPE_DIM), tok),
        pl.BlockSpec((1, tm, POOL_W), tok),
        pl.BlockSpec((1, tm, SGU_W), tok),
    ]
    if not latent:
        out_shape += [jax.ShapeDtypeStruct((B, S, KV_LORA), F32), jax.ShapeDtypeStruct((B, S, ROPE_DIM), F32)]
        out_specs += [pl.BlockSpec((1, tm, KV_LORA), tok), pl.BlockSpec((1, tm, ROPE_DIM), tok)]
    return pl.pallas_call(
        functools.partial(_pre_mixer_kernel, latent=latent, tm=tm),
        out_shape=out_shape,
        grid=(S // tm, B),
        in_specs=in_specs,
        out_specs=out_specs,
        compiler_params=_params(("parallel", "parallel")),
        name="pre_mixer_lat" if latent else "pre_mixer_ctx",
    )(*args)


def _pool_kernel(x_ref, w_ref, scale_ref, o_ref, pad_ref, *, S):
    nslab = POOL_W // LANES
    zeros = jnp.zeros((POOL_PAD, LANES), F32)
    for slab in range(nslab):
        pad_ref[slab, 0:POOL_PAD, :] = zeros
        pad_ref[slab, POOL_PAD + S:POOL_PAD + S + POOL_PAD, :] = zeros
        pad_ref[slab, POOL_PAD:POOL_PAD + S, :] = x_ref[0, :, slab * LANES:(slab + 1) * LANES]

    R = min(POOL_ROWS, S)
    n = R // 2
    lane = lax.broadcasted_iota(jnp.int32, (1, LANES), 1)
    lane_w = lax.broadcasted_iota(jnp.int32, (1, POOL_W), 1)
    half = jnp.where(lane_w < POOL_GC, POOL_WINDOWS[0] // 2,
                     jnp.where(lane_w < 2 * POOL_GC, POOL_WINDOWS[1] // 2,
                               jnp.where(lane_w < 3 * POOL_GC, POOL_WINDOWS[2] // 2, POOL_WINDOWS[3] // 2)))
    for r in range(S // R):
        base = POOL_PAD + r * R

        def window_sums(slab, lo_small, lo_big, par):
            def ld(off):
                return pad_ref[slab, pl.ds(base + par + off, n, stride=2), :]

            x = ld(0)
            small = x
            for off in range(-lo_small, lo_small):
                if off != 0:
                    small = small + ld(off)
            big = small
            for off in list(range(-lo_big, -lo_small)) + list(range(lo_small, lo_big)):
                big = big + ld(off)
            return x, jnp.where(lane < POOL_GC, small, big)

        ds = []
        for par in range(2):
            xa, tot_a = window_sums(0, POOL_WINDOWS[0] // 2, POOL_WINDOWS[1] // 2, par)
            xb, tot_b = window_sums(1, POOL_WINDOWS[2] // 2, POOL_WINDOWS[3] // 2, par)
            t = r * R + par + 2 * lax.broadcasted_iota(jnp.int32, (n, 1), 0)
            cnt = (jnp.minimum(t + half, S) - jnp.maximum(t - half, 0)).astype(F32)
            ds.append(jnp.concatenate([tot_a, tot_b], axis=1) / cnt - jnp.concatenate([xa, xb], axis=1))
        d = pltpu.bitcast(pltpu.pack_elementwise(ds, packed_dtype=BF16), BF16)
        y = _dot(d, w_ref[...]) * scale_ref[...]
        o_ref[0, r * R:(r + 1) * R, :] = y.astype(BF16)


def _pool(pool_in, lw):
    B, S, W = pool_in.shape
    return pl.pallas_call(
        functools.partial(_pool_kernel, S=S),
        out_shape=jax.ShapeDtypeStruct((B, S, W), BF16),
        grid=(B,),
        in_specs=[pl.BlockSpec((1, S, W), lambda b: (b, 0, 0)), _layer_spec(lw['w_pool'], lw['layer']),
                  _layer_spec(lw['pool_scale'], lw['layer'])],
        out_specs=pl.BlockSpec((1, S, W), lambda b: (b, 0, 0)),
        scratch_shapes=[pltpu.VMEM((W // LANES, S + 2 * POOL_PAD, LANES), F32)],
        compiler_params=_params(("parallel",)),
        name="pool",
    )(pool_in, lw['w_pool'], lw['pool_scale'])


def _attention_kernel(q_ref, *refs, tq, T, tc):
    *ckr_refs, wk_ref, wv_ref, o_ref, kt_ref, v_ref, s_ref = refs
    nc = T // tc
    lane = lax.broadcasted_iota(jnp.int32, (1, LANES), 1)
    chunk_src = [(r, k) for r in ckr_refs for k in range(r.shape[1] // tc)]

    @pl.when(pl.program_id(1) == 0)
    def _():
        for c in range(nc):
            src, k = chunk_src[c]
            a = src[0, k * tc:(k + 1) * tc, :]
            kt = lax.dot_general(wk_ref[...], a, (((1,), (1,)), ((), ())), preferred_element_type=F32)
            kt_ref[:, c * tc:(c + 1) * tc] = kt.astype(BF16)
            v_ref[c * tc:(c + 1) * tc, :] = _dot(a[:, :KV_LORA], wv_ref[...]).astype(BF16)

    def scores(h, c, mrun):
        s = _dot(q_ref[0, :, h * HEAD_PAD:(h + 1) * HEAD_PAD],
                 kt_ref[h * HEAD_PAD:(h + 1) * HEAD_PAD, c * tc:(c + 1) * tc])
        s_ref[h % 2, :, c * tc:(c + 1) * tc] = s
        for g in range(tc // LANES):
            mrun = jnp.maximum(mrun, s[:, g * LANES:(g + 1) * LANES])
        return mrun

    def probs_pv(h, c, m, lrun, acc):
        sub = max(LANES, tc // 4)
        parts = []
        for j in range(tc // sub):
            p = jnp.exp2(s_ref[h % 2, :, c * tc + j * sub:c * tc + (j + 1) * sub] - m)
            for g in range(sub // LANES):
                lrun = lrun + p[:, g * LANES:(g + 1) * LANES]
            parts.append(p.astype(BF16))
        hp = h // 2
        acc = acc + _dot(jnp.concatenate(parts, axis=1), v_ref[c * tc:(c + 1) * tc, hp * LANES:(hp + 1) * LANES])
        return lrun, acc

    m_prev = None
    pair_out = None
    for slot in range(MLA_HEADS + 1):
        mrun = jnp.full((tq, LANES), -jnp.inf, F32)
        lrun = jnp.zeros((tq, LANES), F32)
        acc = jnp.zeros((tq, LANES), F32)
        for c in range(nc):
            if slot < MLA_HEADS:
                mrun = scores(slot, c, mrun)
            if slot >= 1:
                lrun, acc = probs_pv(slot - 1, c, m_prev, lrun, acc)
        if slot >= 1:
            h = slot - 1
            o = acc / jnp.sum(lrun, axis=-1, keepdims=True)
            if h % 2 == 0:
                pair_out = o
            else:
                hp = h // 2
                o_ref[0, :, hp * LANES:(hp + 1) * LANES] = jnp.where(lane < V_DIM, pair_out, o).astype(BF16)
        if slot < MLA_HEADS:
            m_prev = jnp.max(mrun, axis=-1, keepdims=True)


def _attention(q, ckr_parts, lw, *, tq, tc):
    B, S, _ = q.shape
    W = ckr_parts[0].shape[-1]
    T = sum(p.shape[1] for p in ckr_parts)
    assert all(p.shape[1] % tc == 0 for p in ckr_parts)
    return pl.pallas_call(
        functools.partial(_attention_kernel, tq=tq, T=T, tc=tc),
        out_shape=jax.ShapeDtypeStruct((B, S, MLA_HEADS * V_DIM), BF16),
        grid=(B, S // tq),
        in_specs=[pl.BlockSpec((1, tq, MLA_HEADS * HEAD_PAD), lambda b, i: (b, i, 0))]
                 + [pl.BlockSpec((1, p.shape[1], W), lambda b, i: (b, 0, 0)) for p in ckr_parts]
                 + [_layer_spec(lw['wk_t'], lw['layer']), _layer_spec(lw['w_v'], lw['layer'])],
        out_specs=pl.BlockSpec((1, tq, MLA_HEADS * V_DIM), lambda b, i: (b, i, 0)),
        scratch_shapes=[pltpu.VMEM((MLA_HEADS * HEAD_PAD, T), BF16), pltpu.VMEM((T, MLA_HEADS * V_DIM), BF16),
                        pltpu.VMEM((2, tq, T), F32)],
        compiler_params=_params(("arbitrary", "arbitrary")),
        name="attention",
    )(q, *ckr_parts, lw['wk_t'], lw['w_v'])


def _post_mixer_kernel(x_ref, attn_ref, pool_ref, sgu_ref, mod_ref, wout_ref, gpost_ref, gffn_ref, x1_ref, h2_ref,
                       *, tm):
    D = x_ref.shape[-1]
    half = tm // 2
    rb = half // (D // MXU_DIM)

    def mix(r0):
        rows = slice(r0, r0 + half)
        return jnp.concatenate([attn_ref[0, rows, :], pool_ref[0, rows, :], sgu_ref[0, rows, :]], axis=-1)

    def finish(out, r0):
        x1 = x_ref[0, r0:r0 + out.shape[0], :] + mod_ref[0, 2:3, :] * _rms(out, gpost_ref[...])
        x1_ref[0, r0:r0 + out.shape[0], :] = x1
        h2 = _rms(x1, gffn_ref[...]) * (1 + mod_ref[0, 4:5, :]) + mod_ref[0, 3:4, :]
        h2_ref[0, r0:r0 + out.shape[0], :] = h2.astype(BF16)

    out_a = _dot(mix(0), wout_ref[...])
    mix_b = mix(half)
    cols = []
    for j in range(D // MXU_DIM):
        cols.append(_dot(mix_b, wout_ref[:, j * MXU_DIM:(j + 1) * MXU_DIM]))
        finish(out_a[j * rb:(j + 1) * rb, :], j * rb)
    finish(jnp.concatenate(cols, axis=-1), half)


def _post_mixer(x, attn, pool, sgu, mod, lw, *, latent, tm):
    B, S, D = x.shape
    mod_idx = (lambda b, i: (b + 1, 0, 0)) if latent else (lambda b, i: (0, 0, 0))
    tok = lambda b, i: (b, i, 0)
    return pl.pallas_call(
        functools.partial(_post_mixer_kernel, tm=tm),
        out_shape=[jax.ShapeDtypeStruct((B, S, D), F32), jax.ShapeDtypeStruct((B, S, D), BF16)],
        grid=(B, S // tm),
        in_specs=[pl.BlockSpec((1, tm, D), tok),
                  pl.BlockSpec((1, tm, attn.shape[-1]), tok),
                  pl.BlockSpec((1, tm, POOL_W), tok),
                  pl.BlockSpec((1, tm, SGU_W), tok),
                  pl.BlockSpec((1, F32_SUBLANES, D), mod_idx),
                  _layer_spec(lw['w_out'], lw['layer']),
                  _layer_spec(lw['g_post_mix'], lw['layer']),
                  _layer_spec(lw['g_pre_ffn'], lw['layer'])],
        out_specs=[pl.BlockSpec((1, tm, D), tok), pl.BlockSpec((1, tm, D), tok)],
        compiler_params=_params(("parallel", "parallel")),
        name="post_mixer",
    )(x, attn, pool, sgu, mod, lw['w_out'], lw['g_post_mix'], lw['g_pre_ffn'])


def _interleave(a, b):
    out, ia, ib = [], 0, 0
    while ia < len(a) or ib < len(b):
        if ib >= len(b) or (ia < len(a) and ia * len(b) <= ib * len(a)):
            out.append(a[ia])
            ia += 1
        else:
            out.append(b[ib])
            ib += 1
    return out


def _conv_ffn_kernel(h_ref, hprev_ref, hnext_ref, x1_ref, mod_ref, wup_ref, cw_ref, wdown_ref, gpost_ref, o_ref,
                     hext_ref, zg0_ref, zv0_ref, zg1_ref, zv1_ref, a0_ref, a1_ref, acc_ref, *, tm, n_chunks, rb):
    i = pl.program_id(1)
    halo = BF16_SUBLANES
    D = hext_ref.shape[1]
    zero = jnp.zeros((), BF16)
    hext_ref[0:halo, :] = jnp.where(i > 0, hprev_ref[0], zero)
    hext_ref[halo:halo + tm, :] = h_ref[0]
    hext_ref[halo + tm:halo + tm + halo, :] = jnp.where(i < pl.num_programs(1) - 1, hnext_ref[0], zero)
    acc_ref[...] = jnp.zeros_like(acc_ref)
    zs = ((zg0_ref, zv0_ref), (zg1_ref, zv1_ref))
    as_ = (a0_ref, a1_ref)

    def conv_pair(z_ref, slab, cw, r0):
        n = rb // 2
        base = halo + r0
        e_m = z_ref[slab, pl.ds(base - 1, n, stride=2), :]
        e_c = z_ref[slab, pl.ds(base, n, stride=2), :]
        e_p = z_ref[slab, pl.ds(base + 1, n, stride=2), :]
        e_pp = z_ref[slab, pl.ds(base + 2, n, stride=2), :]
        w0, w1, w2, b = cw[0:1, :], cw[1:2, :], cw[2:3, :], cw[3:4, :]
        even = e_m * w0 + e_c * w1 + e_p * w2 + b
        odd = e_c * w0 + e_p * w1 + e_pp * w2 + b
        return even, odd

    def up_pieces(c, z_pair):
        pieces = []
        for which in range(2):
            state = {}

            def piece(k, which=which, state=state):
                col = (which * n_chunks + c) * FF_CHUNK
                part = _dot(hext_ref[:, k * MXU_DIM:(k + 1) * MXU_DIM],
                            wup_ref[k * MXU_DIM:(k + 1) * MXU_DIM, col:col + FF_CHUNK])
                state['z'] = part if k == 0 else state['z'] + part
                if k == D // MXU_DIM - 1:
                    for slab in range(FF_CHUNK // LANES):
                        z_pair[which][slab] = state['z'][:, slab * LANES:(slab + 1) * LANES]

            pieces += [functools.partial(piece, k) for k in range(D // MXU_DIM)]
        return pieces

    def gate_pieces(c, z_pair, a_ref):
        def piece(r0):
            for slab in range(FF_CHUNK // LANES):
                ls = slice(slab * LANES, (slab + 1) * LANES)
                col = c * FF_CHUNK + slab * LANES
                cwg = cw_ref[:, col:col + LANES]
                cwv = cw_ref[:, n_chunks * FF_CHUNK + col:n_chunks * FF_CHUNK + col + LANES]
                g_e, g_o = conv_pair(z_pair[0], slab, cwg, r0)
                v_e, v_o = conv_pair(z_pair[1], slab, cwv, r0)
                a_ref[r0 // 2:(r0 + rb) // 2, ls] = pltpu.pack_elementwise(
                    [jax.nn.silu(g_e) * v_e, jax.nn.silu(g_o) * v_o], packed_dtype=BF16)

        return [functools.partial(piece, r * rb) for r in range(tm // rb)]

    def down_pieces(c, a_ref):
        def piece(n):
            ns = slice(n * MXU_DIM, (n + 1) * MXU_DIM)
            acc_ref[:, ns] += _dot(pltpu.bitcast(a_ref[...], BF16), wdown_ref[c * FF_CHUNK:(c + 1) * FF_CHUNK, ns])

        return [functools.partial(piece, n) for n in range(D // MXU_DIM)]

    for s in range(n_chunks + 2):
        par = s % 2
        mxu, vpu = [], []
        if s < n_chunks:
            mxu += up_pieces(s, zs[par])
        if 1 <= s <= n_chunks:
            vpu += gate_pieces(s - 1, zs[1 - par], as_[1 - par])
        if s >= 2:
            mxu += down_pieces(s - 2, as_[par])
        for emit in _interleave(mxu, vpu):
            emit()
    o_ref[0] = x1_ref[0] + mod_ref[0, 5:6, :] * _rms(acc_ref[...], gpost_ref[...])


def _conv_ffn(h2, x1, mod, lw, *, latent, tm):
    B, S, D = x1.shape
    n_chunks = lw['w_down'].shape[1] // FF_CHUNK
    halo = BF16_SUBLANES
    nblk = tm // halo
    last = S // halo - 1
    mod_idx = (lambda b, i: (b + 1, 0, 0)) if latent else (lambda b, i: (0, 0, 0))
    tok = lambda b, i: (b, i, 0)
    z_scratch = pltpu.VMEM((FF_CHUNK // LANES, tm + 2 * halo, LANES), F32)
    a_scratch = pltpu.VMEM((tm // 2, FF_CHUNK), jnp.uint32)
    return pl.pallas_call(
        functools.partial(_conv_ffn_kernel, tm=tm, n_chunks=n_chunks, rb=FF_GATE_ROWS),
        out_shape=jax.ShapeDtypeStruct((B, S, D), F32),
        grid=(B, S // tm),
        in_specs=[pl.BlockSpec((1, tm, D), tok),
                  pl.BlockSpec((1, halo, D), lambda b, i: (b, jnp.maximum(i * nblk - 1, 0), 0)),
                  pl.BlockSpec((1, halo, D), lambda b, i: (b, jnp.minimum((i + 1) * nblk, last), 0)),
                  pl.BlockSpec((1, tm, D), tok),
                  pl.BlockSpec((1, F32_SUBLANES, D), mod_idx),
                  _layer_spec(lw['w_up'], lw['layer']),
                  _layer_spec(lw['conv'], lw['layer']),
                  _layer_spec(lw['w_down'], lw['layer']),
                  _layer_spec(lw['g_post_ffn'], lw['layer'])],
        out_specs=pl.BlockSpec((1, tm, D), tok),
        scratch_shapes=[pltpu.VMEM((tm + 2 * halo, D), BF16), z_scratch, z_scratch, z_scratch, z_scratch,
                        a_scratch, a_scratch, pltpu.VMEM((tm, D), F32)],
        compiler_params=_params(("parallel", "parallel")),
        name="conv_ffn",
    )(h2, h2, h2, x1, mod, lw['w_up'], lw['conv'], lw['w_down'], lw['g_post_ffn'])


def _rope_tables(S):
    n_rows = S // GRID_W
    rows = jnp.repeat(jnp.arange(n_rows, dtype=F32), GRID_W)
    cols = jnp.tile(jnp.arange(GRID_W, dtype=F32), n_rows)
    freqs = ROPE_BASE ** (-jnp.arange(ROPE_FREQS, dtype=F32) / ROPE_FREQS)
    ang = jnp.stack([rows[:, None] * freqs, cols[:, None] * freqs], axis=1)
    cos, sin = jnp.cos(ang), jnp.sin(ang)
    zero = jnp.zeros_like(sin)
    c = jnp.stack([cos, cos], axis=2).reshape(S, ROPE_DIM)
    s_up = jnp.stack([-sin, zero], axis=2).reshape(S, ROPE_DIM)
    s_dn = jnp.stack([zero, sin], axis=2).reshape(S, ROPE_DIM)

    def widen(t, lead, lead_value, scale):
        pads = [jnp.full((S, lead), lead_value, F32), t * scale, jnp.zeros((S, LANES - lead - ROPE_DIM), F32)]
        return jnp.concatenate(pads, axis=1)

    q_tabs = (widen(c, NOPE_DIM, Q_SCALE, Q_SCALE), widen(s_up + s_dn, NOPE_DIM, 0.0, Q_SCALE))
    k_tabs = (widen(c, 0, 0.0, 1.0), widen(s_up, 0, 0.0, 1.0), widen(s_dn, 0, 0.0, 1.0))
    return q_tabs + k_tabs


def _prepare_weights(g_pre_mix, g_post_mix, g_pre_ffn, g_post_ffn, w_in, g_q_a, w_q_b, g_kv_a, w_kv_b,
                     w_pool, pool_scale, g_sgu, w_sgu, b_sgu, w_out, w_up, conv_w, conv_b, w_down):
    depth, D, _ = w_in.shape
    o1 = Q_LORA
    o2 = o1 + KV_LORA
    o3 = o2 + ROPE_DIM
    w_in_r = jnp.concatenate([w_in[..., :o2], w_in[..., o3:], w_in[..., o2:o3],
                              jnp.zeros((depth, D, LANES - ROPE_DIM), F32)], axis=-1)

    qd = NOPE_DIM + ROPE_DIM
    wq = w_q_b.reshape(depth, Q_LORA, MLA_HEADS, qd)
    hz = jnp.zeros((depth, Q_LORA, MLA_HEADS, HEAD_PAD - qd), F32)
    wq_pad = jnp.concatenate([wq, hz], axis=-1).reshape(depth, Q_LORA, MLA_HEADS * HEAD_PAD)
    wq_rope = wq[..., NOPE_DIM:].reshape(depth, Q_LORA, MLA_HEADS, 2, 2, ROPE_FREQS)
    wq_partner = wq_rope[..., ::-1, :].reshape(depth, Q_LORA, MLA_HEADS, ROPE_DIM)
    wq_sw = jnp.concatenate([jnp.zeros((depth, Q_LORA, MLA_HEADS, NOPE_DIM), F32), wq_partner, hz],
                            axis=-1).reshape(depth, Q_LORA, MLA_HEADS * HEAD_PAD)

    wkv = w_kv_b.reshape(depth, KV_LORA, MLA_HEADS, NOPE_DIM + V_DIM)
    wk_t = jnp.transpose(wkv[..., :NOPE_DIM], (0, 2, 3, 1))
    wk_full = jnp.zeros((depth, MLA_HEADS, HEAD_PAD, KV_LORA + ROPE_DIM), F32)
    wk_full = wk_full.at[:, :, :NOPE_DIM, :KV_LORA].set(wk_t)
    wk_full = wk_full.at[:, :, NOPE_DIM:qd, KV_LORA:].set(jnp.eye(ROPE_DIM, dtype=F32))
    w_v = wkv[..., NOPE_DIM:].reshape(depth, KV_LORA, MLA_HEADS * V_DIM)

    wp = jnp.zeros((depth, POOL_W, POOL_W), F32)
    for g in range(len(POOL_WINDOWS)):
        wp = wp.at[:, g * POOL_GC:(g + 1) * POOL_GC, g * POOL_GC:(g + 1) * POOL_GC].set(w_pool[:, g])

    d_ff = w_down.shape[1]
    conv_pad = jnp.zeros((depth, F32_SUBLANES - conv_w.shape[1] - 1, 2 * d_ff), F32)
    conv = jnp.concatenate([conv_w, conv_b[:, None, :], conv_pad], axis=1)
    row = lambda a: a[:, None, :]
    return {
        'g_pre_mix': row(g_pre_mix), 'g_post_mix': row(g_post_mix),
        'g_pre_ffn': row(g_pre_ffn), 'g_post_ffn': row(g_post_ffn),
        'w_in': w_in_r.astype(BF16), 'g_q_a': row(g_q_a),
        'w_q_b': jnp.concatenate([wq_pad, wq_sw], axis=-1).astype(BF16),
        'g_kv_a': row(g_kv_a), 'wk_t': wk_full.reshape(depth, MLA_HEADS * HEAD_PAD, -1).astype(BF16),
        'w_v': w_v.astype(BF16), 'w_pool': wp.astype(BF16), 'pool_scale': row(pool_scale),
        'g_sgu': row(g_sgu), 'w_sgu': w_sgu.astype(BF16),
        'b_sgu': jnp.repeat(jnp.swapaxes(b_sgu, 1, 2), SGU_HD, axis=2),
        'w_out': w_out.astype(BF16), 'w_up': w_up.astype(BF16), 'w_down': w_down.astype(BF16), 'conv': conv,
    }


def _layer(x, mod, lw, tabs, ctx_ckr, *, latent, tm, tq):
    B, S, _ = x.shape
    fold = 1 if latent else max(1, min(B, TOKEN_TILE // S))
    folded = lambda a: a.reshape(B // fold, S * fold, a.shape[-1])
    unfolded = lambda a: a.reshape(B, S, a.shape[-1])
    q, ckr, pool_in, sgu, *cache = map(unfolded, _pre_mixer(folded(x), mod, lw, tabs, latent=latent, tm=tm * fold))
    ckr_parts = [ctx_ckr.astype(BF16), ckr] if latent else [ckr]
    attn = _attention(q, ckr_parts, lw, tq=tq, tc=min(ATTN_KEYS, *[p.shape[1] for p in ckr_parts]))
    pool = _pool(pool_in, lw)
    x1, h2 = map(unfolded, _post_mixer(*map(folded, (x, attn, pool, sgu)), mod, lw, latent=latent, tm=tm * fold))
    return _conv_ffn(h2, x1, mod, lw, latent=latent, tm=tm), cache


def kernel(x_prompt, x_sample, cache_ckv, cache_krope, c, c_ctx, w_mod, b_mod, g_pre_mix, g_post_mix, g_pre_ffn, g_post_ffn, w_in, g_q_a, w_q_b, g_kv_a, w_kv_b, w_pool, pool_scale, g_sgu, w_sgu, b_sgu, w_out, w_up, conv_w, conv_b, w_down):
    depth = w_in.shape[0]
    D = x_prompt.shape[-1]
    mod_rows = -(-(1 + c.shape[0]) // BF16_SUBLANES) * BF16_SUBLANES
    cc = jnp.concatenate([c_ctx[None, :], c, jnp.zeros((mod_rows - 1 - c.shape[0], D), F32)], axis=0)
    mod_all = _modulation(cc, w_mod, b_mod)
    n_vec = mod_all.shape[-1] // D
    mod_all = jnp.pad(mod_all.reshape(depth, mod_rows, n_vec, D), ((0, 0), (0, 0), (0, F32_SUBLANES - n_vec), (0, 0)))
    tabs = _rope_tables(x_sample.shape[1])

    xp, xs = x_prompt, x_sample
    ckv_list, kr_list = [], []
    weights = _prepare_weights(g_pre_mix, g_post_mix, g_pre_ffn, g_post_ffn, w_in, g_q_a, w_q_b, g_kv_a, w_kv_b,
                               w_pool, pool_scale, g_sgu, w_sgu, b_sgu, w_out, w_up, conv_w, conv_b, w_down)
    for l in range(depth):
        lw = dict(weights, layer=l)
        xp, (ckv, kr) = _layer(xp, mod_all[l], lw, None, None, latent=False, tm=xp.shape[1], tq=xp.shape[1])
        ckv_list.append(ckv)
        kr_list.append(kr)
        ctx_ckr = jnp.concatenate([cache_ckv[:, l], cache_krope[:, l]], axis=-1)
        xs, _ = _layer(xs, mod_all[l], lw, tabs, ctx_ckr, latent=True, tm=TOKEN_TILE, tq=ATTN_QUERIES)
    return (xp, xs, jnp.stack(ckv_list, axis=1), jnp.stack(kr_list, axis=1))
```
